```python
import jax, jax.numpy as jnp
from jax import lax
import numpy as np

D_MODEL = 1024
BATCH = 16
SEQ = 4096
DEPTH = 4

CHUNK = 64
N_MEM = 256
N_GROUPS = 4
GROUP = D_MODEL // N_GROUPS
RET_HEADS = 4
RET_HD = GROUP // RET_HEADS
ROPE_BASE = 10000.0
LRU_WIDTH = GROUP
LRU_BLOCKS = 4
LRU_BD = LRU_WIDTH // LRU_BLOCKS
CONV_W = 4
LRU_C = 8.0
GLA_HEADS = 4
GLA_DK = GROUP // 2
GLA_DV = GROUP
GLA_RANK = 16
GLA_TAU = 16.0
HGRN_HEADS = 4
HGRN_HD = GROUP // HGRN_HEADS
XA_HEADS = 4
XA_HD = D_MODEL // XA_HEADS
D_FF = 2816
EPS = 1e-6
IN_SPLITS = (GROUP, GROUP, GROUP, GROUP,
             LRU_WIDTH, LRU_WIDTH,
             GLA_DK, GLA_DK, GLA_DV, GLA_RANK, GLA_DV,
             GROUP, GROUP, GROUP, GROUP)
D_IN = 8 * GROUP + 2 * LRU_WIDTH + 2 * GLA_DK + 2 * GLA_DV + GLA_RANK

kernel_name = "hybrid_chunk_causal_parallel_groups"


def rms_norm(x, g):
    xf = x.astype(jnp.float32)
    y = xf * lax.rsqrt(jnp.mean(xf * xf, axis=-1, keepdims=True) + EPS)
    return (y * g.astype(jnp.float32)).astype(x.dtype)


def head_norm(o):
    of = o.astype(jnp.float32)
    mu = jnp.mean(of, axis=-1, keepdims=True)
    var = jnp.mean(jnp.square(of - mu), axis=-1, keepdims=True)
    return ((of - mu) * lax.rsqrt(var + EPS)).astype(o.dtype)


def swiglu(x, w_gu, w_down):
    g, u = jnp.split(x @ w_gu, 2, axis=-1)
    return (jax.nn.silu(g) * u) @ w_down


def to_heads(t, n_heads):
    b, s, _ = t.shape
    return t.reshape(b, s, n_heads, -1).transpose(0, 2, 1, 3)


def from_heads(t):
    b, h, s, d = t.shape
    return t.transpose(0, 2, 1, 3).reshape(b, s, h * d)


def rope(t, cos, sin):
    t1, t2 = jnp.split(t, 2, axis=-1)
    return jnp.concatenate([t1 * cos - t2 * sin, t1 * sin + t2 * cos], axis=-1)


def chunk_gated_linear_attn(q, k, v, log_f, causal_in_chunk):
    b_, h_, s_, dk = q.shape
    dv = v.shape[-1]
    nc = s_ // CHUNK

    def to_chunks(t):
        return jnp.moveaxis(t.reshape(b_, h_, nc, CHUNK, t.shape[-1]), 2, 0)

    pos = jnp.arange(CHUNK)
    if causal_in_chunk:
        mask = pos[:, None] >= pos[None, :]
    else:
        mask = jnp.ones((CHUNK, CHUNK), dtype=bool)

    def step(state, inp):
        qc, kc, vc, gc = (t.astype(jnp.float32) for t in inp)
        cum = jnp.cumsum(gc, axis=2)
        decay = jnp.exp(-jnp.abs(cum[:, :, :, None, :] - cum[:, :, None, :, :]))
        decay = jnp.where(mask[None, None, :, :, None], decay, 0.0)
        scores = jnp.einsum("bhjd,bhmd,bhjmd->bhjm", qc, kc, decay)
        out = (jnp.einsum("bhjm,bhme->bhje", scores, vc)
               + jnp.einsum("bhjd,bhde->bhje", qc * jnp.exp(cum), state))
        last = cum[:, :, -1, :]
        state = (jnp.exp(last)[..., None] * state
                 + jnp.einsum("bhmd,bhme->bhde", kc * jnp.exp(last[:, :, None, :] - cum), vc))
        return state, out

    state0 = jnp.zeros((b_, h_, dk, dv), jnp.float32)
    _, out = lax.scan(step, state0, (to_chunks(q), to_chunks(k), to_chunks(v), to_chunks(log_f)))
    return jnp.moveaxis(out, 0, 2).reshape(b_, h_, s_, dv).astype(v.dtype)


def retention_group(q, k, v, g, cos, sin):
    qh = rope(to_heads(q, RET_HEADS), cos, sin)
    kh = rope(to_heads(k, RET_HEADS), cos, sin) * (RET_HD ** -0.5)
    vh = to_heads(v, RET_HEADS)
    log_gamma = jnp.log1p(-jnp.exp2(-5.0 - jnp.arange(RET_HEADS, dtype=jnp.float32)))
    log_f = jnp.broadcast_to(log_gamma[None, :, None, None], qh.shape)
    o = chunk_gated_linear_attn(qh, kh, vh, log_f, causal_in_chunk=False)
    return jax.nn.silu(g) * from_heads(head_norm(o))


def rglru_group(xb, gate, conv_w, conv_b, wa, ba, wx, bx, lam):
    b_, s_, w_ = xb.shape
    xc = lax.conv_general_dilated(xb, conv_w[:, None, :], window_strides=(1,),
                                  padding=[(CONV_W - 1, 0)],
                                  dimension_numbers=("NWC", "WIO", "NWC"),
                                  feature_group_count=w_) + conv_b
    xblk = xc.reshape(b_, s_, LRU_BLOCKS, LRU_BD)
    r = jax.nn.sigmoid(jnp.einsum("bsnd,nde->bsne", xblk, wa).reshape(b_, s_, w_) + ba)
    i = jax.nn.sigmoid(jnp.einsum("bsnd,nde->bsne", xblk, wx).reshape(b_, s_, w_) + bx)
    log_a = (LRU_C * r.astype(jnp.float32)) * jax.nn.log_sigmoid(lam.astype(jnp.float32))
    a = jnp.exp(log_a)
    u = jnp.sqrt(-jnp.expm1(2.0 * log_a)) * (i * xc).astype(jnp.float32)

    def combine(left, right):
        a1, b1 = left
        a2, b2 = right
        return a1 * a2, a2 * b1 + b2

    _, h = lax.associative_scan(combine, (a, u), axis=1)
    return h.astype(xb.dtype) * jax.nn.gelu(gate)


def gla_group(q, k, v, a_lr, g, w_a2, b_a):
    qh = to_heads(q, GLA_HEADS) * ((GLA_DK // GLA_HEADS) ** -0.5)
    kh = to_heads(k, GLA_HEADS)
    vh = to_heads(v, GLA_HEADS)
    a_pre = (a_lr @ w_a2 + b_a).astype(jnp.float32)
    log_f = to_heads(jax.nn.log_sigmoid(a_pre) / GLA_TAU, GLA_HEADS)
    o = chunk_gated_linear_attn(qh, kh, vh, log_f, causal_in_chunk=False)
    return jax.nn.silu(g) * from_heads(head_norm(o))


def hgrn2_group(q, f_pre, i, g, lb):
    fp = f_pre.astype(jnp.float32)
    lbf = lb.astype(jnp.float32)
    log_f = jnp.logaddexp(jnp.log(lbf), jnp.log1p(-lbf) + jax.nn.log_sigmoid(fp))
    k = ((1.0 - lbf) * jax.nn.sigmoid(-fp)).astype(q.dtype)
    qh = to_heads(jax.nn.silu(q), HGRN_HEADS)
    o = chunk_gated_linear_attn(qh, to_heads(k, HGRN_HEADS), to_heads(i, HGRN_HEADS),
                                to_heads(log_f, HGRN_HEADS), causal_in_chunk=True)
    return jax.nn.silu(g) * from_heads(head_norm(o))


def cross_attention(h, m, wq, wkv, wo):
    qh = to_heads(h @ wq, XA_HEADS)
    k, v = jnp.split(m @ wkv, 2, axis=-1)
    kh = to_heads(k, XA_HEADS)
    vh = to_heads(v, XA_HEADS)
    s = jnp.einsum("bhqd,bhkd->bhqk", qh, kh).astype(jnp.float32) * (XA_HD ** -0.5)
    p = jax.nn.softmax(s, axis=-1).astype(vh.dtype)
    return from_heads(jnp.einsum("bhqk,bhkd->bhqd", p, vh)) @ wo


def setup_inputs(seed: int = 0) -> dict:
    key = jax.random.key(seed)
    ks = iter(jax.random.split(key, 64))
    L = DEPTH

    def nrm(shape, fan_in):
        return jax.random.normal(next(ks), shape, jnp.float32) * (fan_in ** -0.5)

    def gain(shape):
        return 1.0 + 0.02 * jax.random.normal(next(ks), shape, jnp.float32)

    def small(shape):
        return 0.01 * jax.random.normal(next(ks), shape, jnp.float32)

    x = jax.random.normal(next(ks), (BATCH, SEQ, D_MODEL), jnp.float32)
    mem = jax.random.normal(next(ks), (BATCH, N_MEM, D_MODEL), jnp.float32)
    u = jax.random.uniform(next(ks), (L, LRU_WIDTH), jnp.float32, minval=0.9, maxval=0.999)
    s = u ** (1.0 / LRU_C)
    lru_lambda = jnp.log(s) - jnp.log1p(-s)
    return {
        "x": x,
        "mem": mem,
        "ffn1_norm": gain((L, D_MODEL)),
        "ffn1_w_gu": nrm((L, D_MODEL, 2 * D_FF), D_MODEL),
        "ffn1_w_down": nrm((L, D_FF, D_MODEL), D_FF),
        "mix_norm": gain((L, D_MODEL)),
        "w_in": nrm((L, D_MODEL, D_IN), D_MODEL),
        "w_out": nrm((L, N_GROUPS * GROUP, D_MODEL), N_GROUPS * GROUP),
        "lru_conv_w": nrm((L, CONV_W, LRU_WIDTH), CONV_W),
        "lru_conv_b": small((L, LRU_WIDTH)),
        "lru_wa": nrm((L, LRU_BLOCKS, LRU_BD, LRU_BD), LRU_BD),
        "lru_ba": small((L, LRU_WIDTH)),
        "lru_wx": nrm((L, LRU_BLOCKS, LRU_BD, LRU_BD), LRU_BD),
        "lru_bx": small((L, LRU_WIDTH)),
        "lru_lambda": lru_lambda,
        "gla_w_a2": nrm((L, GLA_RANK, GLA_DK), GLA_RANK),
        "gla_b_a": small((L, GLA_DK)),
        "hgrn_lb_logits": 0.1 * jax.random.normal(next(ks), (L, GROUP), jnp.float32),
        "xattn_norm": gain((L, D_MODEL)),
        "mem_norm": gain((L, D_MODEL)),
        "xattn_wq": nrm((L, D_MODEL, D_MODEL), D_MODEL),
        "xattn_wkv": nrm((L, D_MODEL, 2 * D_MODEL), D_MODEL),
        "xattn_wo": nrm((L, D_MODEL, D_MODEL), D_MODEL),
        "ffn2_norm": gain((L, D_MODEL)),
        "ffn2_w_gu": nrm((L, D_MODEL, 2 * D_FF), D_MODEL),
        "ffn2_w_down": nrm((L, D_FF, D_MODEL), D_FF),
        "final_norm": gain((D_MODEL,)),
    }


def reference(x, mem, ffn1_norm, ffn1_w_gu, ffn1_w_down, mix_norm, w_in, w_out,
              lru_conv_w, lru_conv_b, lru_wa, lru_ba, lru_wx, lru_bx, lru_lambda,
              gla_w_a2, gla_b_a, hgrn_lb_logits, xattn_norm, mem_norm,
              xattn_wq, xattn_wkv, xattn_wo, ffn2_norm, ffn2_w_gu, ffn2_w_down, final_norm):
    seq = x.shape[1]
    inv_freq = ROPE_BASE ** (-jnp.arange(RET_HD // 2, dtype=jnp.float32) / (RET_HD // 2))
    ang = jnp.arange(seq, dtype=jnp.float32)[:, None] * inv_freq[None, :]
    cos = jnp.cos(ang).astype(x.dtype)
    sin = jnp.sin(ang).astype(x.dtype)
    lb_cum = jnp.cumsum(jax.nn.softmax(hgrn_lb_logits.astype(jnp.float32), axis=0), axis=0)
    lb_all = lb_cum - lb_cum[0:1]
    split_at = [int(c) for c in np.cumsum(IN_SPLITS)[:-1]]

    for l in range(DEPTH):
        x = x + 0.5 * swiglu(rms_norm(x, ffn1_norm[l]), ffn1_w_gu[l], ffn1_w_down[l])
        h = rms_norm(x, mix_norm[l])
        (rq, rk, rv, rg, lx, lg, gq, gk, gv, ga, gg, hq, hf, hi, hg) = jnp.split(h @ w_in[l], split_at, axis=-1)
        y = jnp.concatenate([
            retention_group(rq, rk, rv, rg, cos, sin),
            rglru_group(lx, lg, lru_conv_w[l], lru_conv_b[l], lru_wa[l], lru_ba[l],
                        lru_wx[l], lru_bx[l], lru_lambda[l]),
            gla_group(gq, gk, gv, ga, gg, gla_w_a2[l], gla_b_a[l]),
            hgrn2_group(hq, hf, hi, hg, lb_all[l]),
        ], axis=-1)
        x = x + y @ w_out[l]
        x = x + cross_attention(rms_norm(x, xattn_norm[l]), rms_norm(mem, mem_norm[l]),
                                xattn_wq[l], xattn_wkv[l], xattn_wo[l])
        x = x + 0.5 * swiglu(rms_norm(x, ffn2_norm[l]), ffn2_w_gu[l], ffn2_w_down[l])
    return rms_norm(x, final_norm)
```

```python
import functools

import numpy as np
import jax
import jax.numpy as jnp
from jax import lax
from jax.experimental import pallas as pl
from jax.experimental.pallas import tpu as pltpu

F32 = jnp.float32
BF16 = jnp.bfloat16

CHUNK = 64
N_GROUPS = 4
HEADS = 4
ROPE_BASE = 10000.0
CONV_W = 4
LRU_C = 8.0
GLA_RANK = 16
GLA_TAU = 16.0
EPS = 1e-6
LANE = 128

VMEM_LIMIT = 56 * 1024 * 1024


def _rms(x, g):
    return x * lax.rsqrt(jnp.mean(x * x, axis=-1, keepdims=True) + EPS) * g


def _dot(a, b):
    return jnp.dot(a, b, preferred_element_type=F32)


def _dot_nt(a, b):
    return lax.dot_general(a, b, (((1,), (1,)), ((), ())), preferred_element_type=F32)


def _dot_tn(a, b):
    return lax.dot_general(a, b, (((0,), (0,)), ((), ())), preferred_element_type=F32)


def _split_hi_lo(x):
    hi = x.astype(BF16)
    lo = (x - hi.astype(F32)).astype(BF16)
    return hi, lo


def _log_sigmoid(x):
    return jnp.minimum(x, 0.0) - jnp.log1p(jnp.exp(-jnp.abs(x)))


def _neg_expm1(x):
    t = jnp.tanh(0.5 * x)
    return -2.0 * t / (1.0 - t)


def _logaddexp(a, b):
    return jnp.maximum(a, b) + jnp.log1p(jnp.exp(-jnp.abs(a - b)))


def _silu(x):
    return x * jax.nn.sigmoid(x)


def _ffn_body(x_ref, g_ref, wg_ref, wu_ref, wd_ref, *rest, final):
    if final:
        gf_ref, o_ref, h_ref, acc_ref = rest
    else:
        o_ref, h_ref, acc_ref = rest
    j = pl.program_id(1)

    @pl.when(j == 0)
    def _():
        h_ref[...] = _rms(x_ref[...], g_ref[...]).astype(BF16)
        acc_ref[...] = jnp.zeros_like(acc_ref)

    h = h_ref[...]
    gate = _dot(h, wg_ref[...])
    up = _dot(h, wu_ref[...])
    act = (_silu(gate) * up).astype(BF16)
    acc_ref[...] += _dot(act, wd_ref[...])

    @pl.when(j == pl.num_programs(1) - 1)
    def _():
        y = x_ref[...] + 0.5 * acc_ref[...]
        if final:
            y = _rms(y, gf_ref[...])
        o_ref[...] = y


def _ffn(x, g, w_gu, w_down, final_g=None, *, tm=1024, tf=256):
    t, d = x.shape
    f = w_down.shape[0]
    nj = f // tf
    in_specs = [
        pl.BlockSpec((tm, d), lambda i, j: (i, 0)),
        pl.BlockSpec((1, d), lambda i, j: (0, 0)),
        pl.BlockSpec((d, tf), lambda i, j: (0, j)),
        pl.BlockSpec((d, tf), lambda i, j: (0, j + nj)),
        pl.BlockSpec((tf, d), lambda i, j: (j, 0)),
    ]
    args = [x, g.reshape(1, d), w_gu, w_gu, w_down]
    if final_g is not None:
        in_specs.append(pl.BlockSpec((1, d), lambda i, j: (0, 0)))
        args.append(final_g.reshape(1, d))
    return pl.pallas_call(
        functools.partial(_ffn_body, final=final_g is not None),
        grid=(t // tm, nj),
        in_specs=in_specs,
        out_specs=pl.BlockSpec((tm, d), lambda i, j: (i, 0)),
        out_shape=jax.ShapeDtypeStruct((t, d), F32),
        scratch_shapes=[pltpu.VMEM((tm, d), BF16), pltpu.VMEM((tm, d), F32)],
        compiler_params=pltpu.CompilerParams(
            dimension_semantics=("parallel", "arbitrary"), vmem_limit_bytes=VMEM_LIMIT),
        name="ffn_final" if final_g is not None else "ffn",
    )(*args)


def _normproj_body(x_ref, g_ref, w_ref, o_ref):
    h = _rms(x_ref[...], g_ref[...]).astype(BF16)
    o_ref[...] = _dot(h, w_ref[...]).astype(o_ref.dtype)


def _normproj(x, g, w, *, tm=512, name="normproj"):
    t, d = x.shape
    n = w.shape[1]
    return pl.pallas_call(
        _normproj_body,
        grid=(t // tm,),
        in_specs=[
            pl.BlockSpec((tm, d), lambda i: (i, 0)),
            pl.BlockSpec((1, d), lambda i: (0, 0)),
            pl.BlockSpec((d, n), lambda i: (0, 0)),
        ],
        out_specs=pl.BlockSpec((tm, n), lambda i: (i, 0)),
        out_shape=jax.ShapeDtypeStruct((t, n), BF16),
        compiler_params=pltpu.CompilerParams(
            dimension_semantics=("parallel",), vmem_limit_bytes=VMEM_LIMIT),
        name=name,
    )(x, g.reshape(1, d), w)


def _xattn_body(x_ref, y_ref, kv_ref, wout_ref, g_ref, wq_ref, wo_ref, o_ref):
    d = x_ref.shape[-1]
    hd = d // HEADS
    x1 = x_ref[...] + _dot(y_ref[0], wout_ref[...])
    h = _rms(x1, g_ref[...]).astype(BF16)
    q = _dot(h, wq_ref[...]).astype(BF16)
    kv = kv_ref[0]
    outs = []
    for hh in range(HEADS):
        qh = q[:, hh * hd:(hh + 1) * hd]
        kh = kv[:, hh * hd:(hh + 1) * hd]
        vh = kv[:, d + hh * hd:d + (hh + 1) * hd]
        s = _dot_nt(qh, kh) * (hd ** -0.5)
        s = s - jnp.max(s, axis=-1, keepdims=True)
        e = jnp.exp(s)
        p = e / jnp.sum(e, axis=-1, keepdims=True)
        outs.append(_dot(p.astype(BF16), vh).astype(BF16))
    o = jnp.concatenate(outs, axis=-1)
    o_ref[...] = x1 + _dot(o, wo_ref[...])


def _xattn(x, y, kv, w_out, g, wq, wo, *, seq, tm=512):
    t, d = x.shape
    b = t // seq
    ns = seq // tm
    n_mem = kv.shape[1]
    return pl.pallas_call(
        _xattn_body,
        grid=(b, ns),
        in_specs=[
            pl.BlockSpec((tm, d), lambda i, j: (i * ns + j, 0)),
            pl.BlockSpec((1, tm, d), lambda i, j: (i, j, 0)),
            pl.BlockSpec((1, n_mem, 2 * d), lambda i, j: (i, 0, 0)),
            pl.BlockSpec((d, d), lambda i, j: (0, 0)),
            pl.BlockSpec((1, d), lambda i, j: (0, 0)),
            pl.BlockSpec((d, d), lambda i, j: (0, 0)),
            pl.BlockSpec((d, d), lambda i, j: (0, 0)),
        ],
        out_specs=pl.BlockSpec((tm, d), lambda i, j: (i * ns + j, 0)),
        out_shape=jax.ShapeDtypeStruct((t, d), F32),
        compiler_params=pltpu.CompilerParams(
            dimension_semantics=("parallel", "parallel"), vmem_limit_bytes=VMEM_LIMIT),
        name="outproj_xattn",
    )(x, y, kv, w_out, g.reshape(1, d), wq, wo)


def _level_widths(sb):
    ws, w = [], sb // 2
    while w >= 1:
        ws.append(w)
        w //= 2
    return ws


def _mixer_consts(sb, group):
    r = np.arange(sb)
    ws = _level_widths(sb)
    seg, mask_c, mask_n = [], [], []
    for w in ws:
        start = (r // (2 * w)) * (2 * w)
        split = start + w
        upper = r >= split
        m = np.zeros((sb, sb), np.float32)
        for j in range(sb):
            if upper[j]:
                m[j, split[j]:j + 1] = 1.0
            else:
                m[j, j + 1:split[j]] = 1.0
        seg.append(m)
        same = start[:, None] == start[None, :]
        causal = same & upper[:, None] & ~upper[None, :]
        anti = same & ~upper[:, None] & upper[None, :]
        mask_c.append(causal.astype(np.float32))
        mask_n.append((causal | (anti & (2 * w <= CHUNK))).astype(np.float32))
    tri_incl = (r[:, None] >= r[None, :]).astype(np.float32)
    tri_after = (r[None, :] > r[:, None]).astype(np.float32)
    seg_all = np.concatenate(seg + [tri_incl, tri_after], axis=0)

    hd = group // HEADS
    ch = np.arange(group)
    head_of = ch // hd
    avg = (head_of[:, None] == head_of[None, :]).astype(np.float32) / hd
    hm64 = (head_of[None, :] == np.arange(HEADS)[:, None]).astype(np.float32)[:, None, :]
    ck = np.arange(group // 2)
    head_of_k = ck // (hd // 2)
    hm32 = (head_of_k[None, :] == np.arange(HEADS)[:, None]).astype(np.float32)[:, None, :]
    bd_gla_t = (head_of[:, None] == head_of_k[None, :]).astype(np.float32)
    expand_gla = bd_gla_t.T.copy()

    log_gamma = np.log1p(-np.exp2(-5.0 - np.arange(HEADS, dtype=np.float64)))
    cj, cm = r[:, None] // CHUNK, r[None, :] // CHUNK
    dist = np.where(cj == cm, np.abs(r[:, None] - r[None, :]), r[:, None] - r[None, :])
    ret_mask = np.stack([np.where(cm <= cj, np.exp(dist * lg), 0.0) for lg in log_gamma])
    lg_ch = log_gamma[head_of]
    ret_qdec = np.exp((r[:, None] + 1) * lg_ch[None, :])
    ret_kdec = np.exp((sb - 1 - r[:, None]) * lg_ch[None, :])
    ret_sdec = np.exp(sb * lg_ch)[None, :]
    first_half = ((ch % hd) < hd // 2).astype(np.float32)[None, :]
    return dict(
        seg_all=seg_all, mask_c=np.stack(mask_c), mask_n=np.stack(mask_n),
        avg=avg, hm64=hm64, hm32=hm32, bd64=(avg > 0).astype(np.float32),
        bd_gla_t=bd_gla_t, expand_gla=expand_gla,
        ret_mask=ret_mask.astype(np.float32), ret_qdec=ret_qdec.astype(np.float32),
        ret_kdec=ret_kdec.astype(np.float32), ret_sdec=ret_sdec.astype(np.float32),
        first_half=first_half,
    )


def _head_norm(o, avg_bf):
    hi, lo = _split_hi_lo(o)
    mu = _dot(hi, avg_bf) + _dot(lo, avg_bf)
    dlt = o - mu
    hi, lo = _split_hi_lo(dlt * dlt)
    var = _dot(hi, avg_bf) + _dot(lo, avg_bf)
    return dlt * lax.rsqrt(var + EPS)


def _gated_scores(q, k, logf, seg_ref, mask_ref, hm_ref, d_ref, s_ref, n_levels, sb):
    hi, lo = _split_hi_lo(logf)
    seg = seg_ref[...]
    d_ref[:, :logf.shape[1]] = _dot(seg, hi) + _dot(seg, lo)
    dk = logf.shape[1]
    for lvl in range(n_levels):
        e = jnp.exp(d_ref[lvl * sb:(lvl + 1) * sb, :dk])
        qs = q * e
        ks = (k * e).astype(BF16)
        mask = mask_ref[lvl]
        for hh in range(HEADS):
            sc = _dot_nt((qs * hm_ref[hh]).astype(BF16), ks) * mask
            if lvl == 0:
                s_ref[hh] = sc
            else:
                s_ref[hh] += sc
    b_incl = d_ref[n_levels * sb:(n_levels + 1) * sb, :dk]
    b_after = d_ref[(n_levels + 1) * sb:(n_levels + 2) * sb, :dk]
    return b_incl, b_after


def _mixer_body(p_ref, cos_ref, sin_ref,
                seg_ref, maskc_ref, maskn_ref, avg_ref, hm64_ref, hm32_ref, bd64_ref,
                bdgt_ref, expg_ref, rmask_ref, rqdec_ref, rkdec_ref, rsdec_ref, fh_ref,
                convw_ref, convb_ref, wa_ref, ba_ref, wx_ref, bx_ref, lam_ref,
                wa2_ref, bga_ref, lbl_ref,
                y_ref,
                d_ref, s_ref, ret_st, gla_st, hgrn_st, lru_h, lru_prev,
                *, sb, group, layer):
    g = group
    n_levels = len(_level_widths(sb))
    sidx = pl.program_id(1)

    @pl.when(sidx == 0)
    def _():
        ret_st[...] = jnp.zeros_like(ret_st)
        gla_st[...] = jnp.zeros_like(gla_st)
        hgrn_st[...] = jnp.zeros_like(hgrn_st)
        lru_h[...] = jnp.zeros_like(lru_h)
        lru_prev[...] = jnp.zeros_like(lru_prev)

    def col(i, width=g):
        return p_ref[0, :, i:i + width].astype(F32)

    avg_bf = avg_ref[...]
    row = lax.broadcasted_iota(jnp.int32, (sb, g), 0)

    c0 = 0
    rq, rk, rv, rg = col(c0), col(c0 + g), col(c0 + 2 * g), col(c0 + 3 * g)
    cos, sin = cos_ref[...], sin_ref[...]
    fh = fh_ref[...] > 0.5
    hd = g // HEADS

    def rope(t):
        swapped = jnp.where(fh, pltpu.roll(t, g - hd // 2, 1), pltpu.roll(t, hd // 2, 1))
        return t * cos + swapped * sin

    qr = rope(rq)
    kr = (rope(rk) * (hd ** -0.5))
    kr_bf = kr.astype(BF16)
    o = _dot_nt((qr * rqdec_ref[...]).astype(BF16), ret_st[...].astype(BF16))
    for hh in range(HEADS):
        sc = _dot_nt((qr * hm64_ref[hh]).astype(BF16), kr_bf) * rmask_ref[hh]
        o += _dot(sc.astype(BF16), (rv * hm64_ref[hh]).astype(BF16))
    ret_st[...] = (ret_st[...] * rsdec_ref[...]
                   + _dot_tn(rv.astype(BF16), (kr * rkdec_ref[...]).astype(BF16)) * bd64_ref[...])
    y_ref[0, :, 0:g] = (_silu(rg) * _head_norm(o, avg_bf)).astype(y_ref.dtype)

    c0 = 4 * g
    lx, lg = col(c0), col(c0 + g)
    prev = lru_prev[...]
    xc = lx * convw_ref[CONV_W - 1:CONV_W, :] + convb_ref[...]
    for sh in range(1, CONV_W):
        shifted = jnp.where(row >= sh, pltpu.roll(lx, sh, 0), pltpu.roll(prev, sh, 0))
        xc += shifted * convw_ref[CONV_W - 1 - sh:CONV_W - sh, :]
    lru_prev[...] = lx
    xc_bf = xc.astype(BF16)
    rgate = jax.nn.sigmoid(_dot(xc_bf, wa_ref[...]) + ba_ref[...])
    igate = jax.nn.sigmoid(_dot(xc_bf, wx_ref[...]) + bx_ref[...])
    log_a = (LRU_C * rgate) * _log_sigmoid(lam_ref[...])
    a = jnp.exp(log_a)
    u = jnp.sqrt(_neg_expm1(2.0 * log_a)) * (igate * xc)
    sh = 1
    while sh < sb:
        keep = row >= sh
        a_sh = jnp.where(keep, pltpu.roll(a, sh, 0), 1.0)
        u_sh = jnp.where(keep, pltpu.roll(u, sh, 0), 0.0)
        u = u + a * u_sh
        a = a * a_sh
        sh *= 2
    hseq = u + a * lru_h[...]
    lru_h[...] = hseq[sb - 1:sb, :]
    y_ref[0, :, g:2 * g] = (hseq * jax.nn.gelu(lg, approximate=True)).astype(y_ref.dtype)

    c0 = 6 * g
    dk = g // 2
    gq = col(c0, dk) * ((dk // HEADS) ** -0.5)
    gk = col(c0 + dk, dk)
    gv = col(c0 + 2 * dk)
    gg = col(c0 + 2 * dk + g)
    ga = p_ref[0, :, 13 * g:13 * g + LANE]
    a_pre = _dot(ga, wa2_ref[...]) + bga_ref[...]
    logf = _log_sigmoid(a_pre) / GLA_TAU
    b_incl, b_after = _gated_scores(gq, gk, logf, seg_ref, maskn_ref, hm32_ref,
                                    d_ref, s_ref, n_levels, sb)
    o = _dot((gq * gk).astype(BF16), expg_ref[...]) * gv
    o += _dot_nt((gq * jnp.exp(b_incl)).astype(BF16), gla_st[...].astype(BF16))
    for hh in range(HEADS):
        o += _dot(s_ref[hh].astype(BF16), (gv * hm64_ref[hh]).astype(BF16))
    gla_st[...] = (gla_st[...] * jnp.exp(b_incl[sb - 1:sb, :])
                   + _dot_tn(gv.astype(BF16), (gk * jnp.exp(b_after)).astype(BF16)) * bdgt_ref[...])
    y_ref[0, :, 2 * g:3 * g] = (_silu(gg) * _head_norm(o, avg_bf)).astype(y_ref.dtype)

    c0 = 9 * g
    hq, hf, hi_, hg = col(c0), col(c0 + g), col(c0 + 2 * g), col(c0 + 3 * g)
    logits = lbl_ref[...]
    ex = jnp.exp(logits - jnp.max(logits, axis=0, keepdims=True))
    sm = ex / jnp.sum(ex, axis=0, keepdims=True)
    lb = jnp.zeros((1, g), F32)
    for i in range(1, layer + 1):
        lb = lb + sm[i:i + 1, :]
    logf = _logaddexp(jnp.log(lb), jnp.log1p(-lb) + _log_sigmoid(hf))
    hk = (1.0 - lb) * jax.nn.sigmoid(-hf)
    hqs = _silu(hq)
    b_incl, b_after = _gated_scores(hqs, hk, logf, seg_ref, maskc_ref, hm64_ref,
                                    d_ref, s_ref, n_levels, sb)
    o = _dot((hqs * hk).astype(BF16), bd64_ref[...].astype(BF16)) * hi_
    o += _dot_nt((hqs * jnp.exp(b_incl)).astype(BF16), hgrn_st[...].astype(BF16))
    for hh in range(HEADS):
        o += _dot(s_ref[hh].astype(BF16), (hi_ * hm64_ref[hh]).astype(BF16))
    hgrn_st[...] = (hgrn_st[...] * jnp.exp(b_incl[sb - 1:sb, :])
                    + _dot_tn(hi_.astype(BF16), (hk * jnp.exp(b_after)).astype(BF16)) * bd64_ref[...])
    y_ref[0, :, 3 * g:4 * g] = (_silu(hg) * _head_norm(o, avg_bf)).astype(y_ref.dtype)


def _mixers(proj, cos_t, sin_t, consts, lw, *, layer, sb, group):
    b, s, n = proj.shape
    g = group
    n_seg = consts["seg_all"].shape[0]

    def full(a):
        nd = a.ndim
        return pl.BlockSpec(a.shape, lambda i, j, nd=nd: (0,) * nd)

    const_args = [
        consts["seg_all"].astype(BF16), consts["mask_c"], consts["mask_n"],
        consts["avg"].astype(BF16), consts["hm64"], consts["hm32"], consts["bd64"],
        consts["bd_gla_t"], consts["expand_gla"].astype(BF16), consts["ret_mask"],
        consts["ret_qdec"], consts["ret_kdec"], consts["ret_sdec"], consts["first_half"],
    ]
    layer_args = [
        lw["conv_w"], lw["conv_b"], lw["wa_bd"], lw["ba"], lw["wx_bd"], lw["bx"], lw["lam"],
        lw["w_a2"], lw["b_a"], lw["lb_logits"],
    ]
    in_specs = ([pl.BlockSpec((1, sb, n), lambda i, j: (i, j, 0)),
                 pl.BlockSpec((sb, g), lambda i, j: (j, 0)),
                 pl.BlockSpec((sb, g), lambda i, j: (j, 0))]
                + [full(a) for a in const_args] + [full(a) for a in layer_args])
    return pl.pallas_call(
        functools.partial(_mixer_body, sb=sb, group=g, layer=layer),
        grid=(b, s // sb),
        in_specs=in_specs,
        out_specs=pl.BlockSpec((1, sb, N_GROUPS * g), lambda i, j: (i, j, 0)),
        out_shape=jax.ShapeDtypeStruct((b, s, N_GROUPS * g), BF16),
        scratch_shapes=[
            pltpu.VMEM((n_seg, g), F32),
            pltpu.VMEM((HEADS, sb, sb), F32),
            pltpu.VMEM((g, g), F32),
            pltpu.VMEM((g, g // 2), F32),
            pltpu.VMEM((g, g), F32),
            pltpu.VMEM((1, g), F32),
            pltpu.VMEM((sb, g), F32),
        ],
        compiler_params=pltpu.CompilerParams(
            dimension_semantics=("parallel", "arbitrary"), vmem_limit_bytes=VMEM_LIMIT),
        name="mixers",
    )(proj, cos_t, sin_t, *const_args, *layer_args)


def _block_diag(w):
    nb, bd, _ = w.shape
    out = jnp.zeros((nb * bd, nb * bd), w.dtype)
    for i in range(nb):
        out = out.at[i * bd:(i + 1) * bd, i * bd:(i + 1) * bd].set(w[i])
    return out


def _pad_cols(w, to):
    return jnp.pad(w, ((0, 0), (0, to - w.shape[1])))


def kernel(x, mem, ffn1_norm, ffn1_w_gu, ffn1_w_down, mix_norm, w_in, w_out, lru_conv_w, lru_conv_b, lru_wa, lru_ba, lru_wx, lru_bx, lru_lambda, gla_w_a2, gla_b_a, hgrn_lb_logits, xattn_norm, mem_norm, xattn_wq, xattn_wkv, xattn_wo, ffn2_norm, ffn2_w_gu, ffn2_w_down, final_norm):
    bsz, seq, d = x.shape
    depth = w_in.shape[0]
    n_mem = mem.shape[1]
    g = d // N_GROUPS
    sb = min(256, seq)
    t = bsz * seq

    hd = g // HEADS
    inv_freq = ROPE_BASE ** (-jnp.arange(hd // 2, dtype=F32) / (hd // 2))
    ang = jnp.arange(seq, dtype=F32)[:, None] * inv_freq[None, :]
    cos_t = jnp.tile(jnp.cos(ang), (1, 2 * HEADS))
    sin_t = jnp.tile(jnp.concatenate([-jnp.sin(ang), jnp.sin(ang)], axis=-1), (1, HEADS))

    consts = {k: jnp.asarray(v) for k, v in _mixer_consts(sb, g).items()}

    ga0 = 4 * g + 2 * g + g // 2 + g // 2 + g
    ga1 = ga0 + GLA_RANK
    bf = lambda a: a.astype(BF16)
    xf = x.reshape(t, d)
    memf = mem.reshape(bsz * n_mem, d)
    for l in range(depth):
        w_in_l = jnp.concatenate(
            [w_in[l][:, :ga0], w_in[l][:, ga1:], _pad_cols(w_in[l][:, ga0:ga1], LANE)], axis=1)
        lw = dict(
            conv_w=lru_conv_w[l], conv_b=lru_conv_b[l].reshape(1, g),
            wa_bd=bf(_block_diag(lru_wa[l])), ba=lru_ba[l].reshape(1, g),
            wx_bd=bf(_block_diag(lru_wx[l])), bx=lru_bx[l].reshape(1, g),
            lam=lru_lambda[l].reshape(1, g),
            w_a2=bf(jnp.pad(gla_w_a2[l], ((0, LANE - GLA_RANK), (0, 0)))),
            b_a=gla_b_a[l].reshape(1, g // 2),
            lb_logits=hgrn_lb_logits,
        )
        xf = _ffn(xf, ffn1_norm[l], bf(ffn1_w_gu[l]), bf(ffn1_w_down[l]))
        proj = _normproj(xf, mix_norm[l], bf(w_in_l), name="in_proj")
        y = _mixers(proj.reshape(bsz, seq, -1), cos_t, sin_t, consts, lw,
                    layer=l, sb=sb, group=g)
        kv = _normproj(memf, mem_norm[l], bf(xattn_wkv[l]), name="mem_kv")
        xf = _xattn(xf, y, kv.reshape(bsz, n_mem, 2 * d), bf(w_out[l]), xattn_norm[l],
                    bf(xattn_wq[l]), bf(xattn_wo[l]), seq=seq)
        xf = _ffn(xf, ffn2_norm[l], bf(ffn2_w_gu[l]), bf(ffn2_w_down[l]),
                  final_g=final_norm if l == depth - 1 else None)
    return xf.reshape(bsz, seq, d)
```

```python
import functools

import numpy as np
import jax
import jax.numpy as jnp
from jax import lax
from jax.experimental import pallas as pl
from jax.experimental.pallas import tpu as pltpu

F32 = jnp.float32
BF16 = jnp.bfloat16

CHUNK = 64
N_GROUPS = 4
HEADS = 4
ROPE_BASE = 10000.0
CONV_W = 4
LRU_C = 8.0
GLA_RANK = 16
GLA_TAU = 16.0
EPS = 1e-6
LANE = 128

VMEM_LIMIT = 56 * 1024 * 1024


def _rms(x, g):
    return x * lax.rsqrt(jnp.mean(x * x, axis=-1, keepdims=True) + EPS) * g


def _dot(a, b):
    return jnp.dot(a, b, preferred_element_type=F32)


def _dot_nt(a, b):
    return lax.dot_general(a, b, (((1,), (1,)), ((), ())), preferred_element_type=F32)


def _dot_tn(a, b):
    return lax.dot_general(a, b, (((0,), (0,)), ((), ())), preferred_element_type=F32)


def _split_hi_lo(x):
    hi = x.astype(BF16)
    lo = (x - hi.astype(F32)).astype(BF16)
    return hi, lo


def _log_sigmoid(x):
    return jnp.minimum(x, 0.0) - jnp.log1p(jnp.exp(-jnp.abs(x)))


def _neg_expm1(x):
    t = jnp.tanh(0.5 * x)
    return -2.0 * t / (1.0 - t)


def _logaddexp(a, b):
    return jnp.maximum(a, b) + jnp.log1p(jnp.exp(-jnp.abs(a - b)))


def _silu(x):
    return x * jax.nn.sigmoid(x)


def _ffn_body(x_ref, g_ref, wg_ref, wu_ref, wd_ref, *rest, final):
    if final:
        gf_ref, o_ref, h_ref, acc_ref = rest
    else:
        o_ref, h_ref, acc_ref = rest
    j = pl.program_id(1)

    @pl.when(j == 0)
    def _():
        h_ref[...] = _rms(x_ref[...], g_ref[...]).astype(BF16)
        acc_ref[...] = jnp.zeros_like(acc_ref)

    h = h_ref[...]
    gate = _dot(h, wg_ref[...])
    up = _dot(h, wu_ref[...])
    act = (_silu(gate) * up).astype(BF16)
    acc_ref[...] += _dot(act, wd_ref[...])

    @pl.when(j == pl.num_programs(1) - 1)
    def _():
        y = x_ref[...] + 0.5 * acc_ref[...]
        if final:
            y = _rms(y, gf_ref[...])
        o_ref[...] = y


def _ffn(x, g, w_gu, w_down, final_g=None, *, tm=1024, tf=256):
    t, d = x.shape
    f = w_down.shape[0]
    nj = f // tf
    in_specs = [
        pl.BlockSpec((tm, d), lambda i, j: (i, 0)),
        pl.BlockSpec((1, d), lambda i, j: (0, 0)),
        pl.BlockSpec((d, tf), lambda i, j: (0, j)),
        pl.BlockSpec((d, tf), lambda i, j: (0, j + nj)),
        pl.BlockSpec((tf, d), lambda i, j: (j, 0)),
    ]
    args = [x, g.reshape(1, d), w_gu, w_gu, w_down]
    if final_g is not None:
        in_specs.append(pl.BlockSpec((1, d), lambda i, j: (0, 0)))
        args.append(final_g.reshape(1, d))
    return pl.pallas_call(
        functools.partial(_ffn_body, final=final_g is not None),
        grid=(t // tm, nj),
        in_specs=in_specs,
        out_specs=pl.BlockSpec((tm, d), lambda i, j: (i, 0)),
        out_shape=jax.ShapeDtypeStruct((t, d), F32),
        scratch_shapes=[pltpu.VMEM((tm, d), BF16), pltpu.VMEM((tm, d), F32)],
        compiler_params=pltpu.CompilerParams(
            dimension_semantics=("parallel", "arbitrary"), vmem_limit_bytes=VMEM_LIMIT),
        name="ffn_final" if final_g is not None else "ffn",
    )(*args)


def _normproj_body(x_ref, g_ref, w_ref, o_ref):
    h = _rms(x_ref[...], g_ref[...]).astype(BF16)
    o_ref[...] = _dot(h, w_ref[...]).astype(o_ref.dtype)


def _normproj(x, g, w, *, tm=512, name="normproj"):
    t, d = x.shape
    n = w.shape[1]
    return pl.pallas_call(
        _normproj_body,
        grid=(t // tm,),
        in_specs=[
            pl.BlockSpec((tm, d), lambda i: (i, 0)),
            pl.BlockSpec((1, d), lambda i: (0, 0)),
            pl.BlockSpec((d, n), lambda i: (0, 0)),
        ],
        out_specs=pl.BlockSpec((tm, n), lambda i: (i, 0)),
        out_shape=jax.ShapeDtypeStruct((t, n), BF16),
        compiler_params=pltpu.CompilerParams(
            dimension_semantics=("parallel",), vmem_limit_bytes=VMEM_LIMIT),
        name=name,
    )(x, g.reshape(1, d), w)


def _xattn_body(x_ref, y_ref, kv_ref, wout_ref, g_ref, wq_ref, wo_ref, o_ref):
    d = x_ref.shape[-1]
    hd = d // HEADS
    x1 = x_ref[...] + _dot(y_ref[0], wout_ref[...])
    h = _rms(x1, g_ref[...]).astype(BF16)
    q = _dot(h, wq_ref[...]).astype(BF16)
    kv = kv_ref[0]
    outs = []
    for hh in range(HEADS):
        qh = q[:, hh * hd:(hh + 1) * hd]
        kh = kv[:, hh * hd:(hh + 1) * hd]
        vh = kv[:, d + hh * hd:d + (hh + 1) * hd]
        s = _dot_nt(qh, kh) * (hd ** -0.5)
        s = s - jnp.max(s, axis=-1, keepdims=True)
        e = jnp.exp(s)
        p = e / jnp.sum(e, axis=-1, keepdims=True)
        outs.append(_dot(p.astype(BF16), vh).astype(BF16))
    o = jnp.concatenate(outs, axis=-1)
    o_ref[...] = x1 + _dot(o, wo_ref[...])


def _xattn(x, y, kv, w_out, g, wq, wo, *, seq, tm=512):
    t, d = x.shape
    b = t // seq
    ns = seq // tm
    n_mem = kv.shape[1]
    return pl.pallas_call(
        _xattn_body,
        grid=(b, ns),
        in_specs=[
            pl.BlockSpec((tm, d), lambda i, j: (i * ns + j, 0)),
            pl.BlockSpec((1, tm, d), lambda i, j: (i, j, 0)),
            pl.BlockSpec((1, n_mem, 2 * d), lambda i, j: (i, 0, 0)),
            pl.BlockSpec((d, d), lambda i, j: (0, 0)),
            pl.BlockSpec((1, d), lambda i, j: (0, 0)),
            pl.BlockSpec((d, d), lambda i, j: (0, 0)),
            pl.BlockSpec((d, d), lambda i, j: (0, 0)),
        ],
        out_specs=pl.BlockSpec((tm, d), lambda i, j: (i * ns + j, 0)),
        out_shape=jax.ShapeDtypeStruct((t, d), F32),
        compiler_params=pltpu.CompilerParams(
            dimension_semantics=("parallel", "parallel"), vmem_limit_bytes=VMEM_LIMIT),
        name="outproj_xattn",
    )(x, y, kv, w_out, g.reshape(1, d), wq, wo)


LOG2_E = 1.4426950408889634
SEG_WIDTHS = (2,)


def _level_widths(sb):
    ws, w = [], sb // 2
    while w >= 1:
        ws.append(w)
        w //= 2
    return ws


def _mixer_consts(sb, group):
    r = np.arange(sb)
    ws = _level_widths(sb)
    seg, mask_c, mask_n = [], [], []
    for w in ws:
        start = (r // (2 * w)) * (2 * w)
        split = start + w
        upper = r >= split
        m = np.zeros((sb, sb), np.float32)
        for j in range(sb):
            if upper[j]:
                m[j, split[j]:j + 1] = 1.0
            else:
                m[j, j + 1:split[j]] = 1.0
        seg.append(m)
        same = start[:, None] == start[None, :]
        causal = same & upper[:, None] & ~upper[None, :]
        anti = same & ~upper[:, None] & upper[None, :]
        mask_c.append(causal.astype(np.float32))
        mask_n.append((causal | (anti & (2 * w <= CHUNK))).astype(np.float32))
    tri_incl = (r[:, None] >= r[None, :]).astype(np.float32)
    seg_all = np.concatenate([seg[ws.index(w)] for w in SEG_WIDTHS] + [tri_incl], axis=0)

    hd = group // HEADS
    ch = np.arange(group)
    head_of = ch // hd
    avg = (head_of[:, None] == head_of[None, :]).astype(np.float32) / hd
    hm64 = (head_of[None, :] == np.arange(HEADS)[:, None]).astype(np.float32)[:, None, :]
    ck = np.arange(group // 2)
    head_of_k = ck // (hd // 2)
    hm32 = (head_of_k[None, :] == np.arange(HEADS)[:, None]).astype(np.float32)[:, None, :]
    bd_gla_t = (head_of[:, None] == head_of_k[None, :]).astype(np.float32)
    expand_gla = bd_gla_t.T.copy()

    log_gamma = np.log1p(-np.exp2(-5.0 - np.arange(HEADS, dtype=np.float64)))
    cj, cm = r[:, None] // CHUNK, r[None, :] // CHUNK
    dist = np.where(cj == cm, np.abs(r[:, None] - r[None, :]), r[:, None] - r[None, :])
    ret_mask = np.stack([np.where(cm <= cj, np.exp(dist * lg), 0.0) for lg in log_gamma])
    lg_ch = log_gamma[head_of]
    ret_qdec = np.exp((r[:, None] + 1) * lg_ch[None, :])
    ret_kdec = np.exp((sb - 1 - r[:, None]) * lg_ch[None, :])
    ret_sdec = np.exp(sb * lg_ch)[None, :]
    first_half = ((ch % hd) < hd // 2).astype(np.float32)[None, :]
    return dict(
        seg_all=seg_all, mask_c=np.stack(mask_c), mask_n=np.stack(mask_n),
        avg=avg, hm64=hm64, hm32=hm32, bd64=(avg > 0).astype(np.float32),
        bd_gla_t=bd_gla_t, expand_gla=expand_gla,
        ret_mask=ret_mask.astype(np.float32), ret_qdec=ret_qdec.astype(np.float32),
        ret_kdec=ret_kdec.astype(np.float32), ret_sdec=ret_sdec.astype(np.float32),
        first_half=first_half,
    )


def _head_norm(o, avg_bf):
    mu = _dot(o.astype(BF16), avg_bf)
    dlt = o - mu
    var = _dot((dlt * dlt).astype(BF16), avg_bf)
    return dlt * lax.rsqrt(var + EPS)


def _score_blocks(w, sb, causal):
    if causal and 2 * w == sb and w % LANE == 0:
        return [(w, sb, 0, w, False)]
    if causal and w >= 8:
        return [(gs + w, gs + 2 * w, (gs // LANE) * LANE, (gs // LANE + 1) * LANE, True)
                for gs in range(0, sb, 2 * w)]
    return [(t, t + LANE, t, t + LANE, True) for t in range(0, sb, LANE)]


def _gated_scores(q, k, logf, seg_ref, mask_ref, hm_ref, b_ref, s_ref, sb, causal_in_chunk,
                  filler=None):
    dk = logf.shape[1]
    logf = logf * LOG2_E
    hi, lo = _split_hi_lo(logf)
    both = _dot(seg_ref[...], jnp.concatenate([hi, lo], axis=1))
    sums = both[:, :dk] + both[:, dk:]
    fine = {w: sums[i * sb:(i + 1) * sb] for i, w in enumerate(SEG_WIDTHS)}
    b_incl = sums[len(SEG_WIDTHS) * sb:]
    b_ref[:, :dk] = b_incl
    odd = lax.broadcasted_iota(jnp.int32, (sb, dk), 0) % 2 == 1
    s_ref[...] = jnp.zeros_like(s_ref)
    for lvl, w in enumerate(_level_widths(sb)):
        if w in fine:
            d = fine[w]
        elif w == 1:
            d = jnp.where(odd, logf, 0.0)
        else:
            beta = jnp.concatenate(
                [jnp.broadcast_to(b_ref[gs + w - 1:gs + w, :dk], (2 * w, dk))
                 for gs in range(0, sb, 2 * w)], axis=0)
            d = -jnp.abs(b_incl - beta)
        e = jnp.exp2(d)
        qs = (q * e).astype(BF16)
        ks = (k * e).astype(BF16)
        blocks = _score_blocks(w, sb, causal_in_chunk or 2 * w > CHUNK)
        for hh in range(HEADS):
            sc = _dot_nt(qs * hm_ref[hh], ks)
            for r0, r1, c0, c1, masked in blocks:
                part = sc[r0:r1, c0:c1]
                if masked:
                    part = part * mask_ref[lvl, r0:r1, c0:c1]
                s_ref[hh, r0:r1, c0:c1] += part
        if filler is not None:
            next(filler, None)
    b_after = b_ref[sb - 1:sb, :dk] - b_incl
    return b_incl, b_after


def _mixer_body(p_ref, cos_ref, sin_ref,
                seg_ref, maskc_ref, maskn_ref, avg_ref, hm64_ref, hm32_ref, bd64_ref,
                bdgt_ref, expg_ref, rmask_ref, rqdec_ref, rkdec_ref, rsdec_ref, fh_ref,
                convw_ref, convb_ref, wa_ref, ba_ref, wx_ref, bx_ref, lam_ref,
                wa2_ref, bga_ref, lbl_ref,
                y_ref,
                b_ref, s_ref, ret_st, gla_st, hgrn_st, lru_h, lru_prev,
                *, sb, group, layer):
    g = group
    sidx = pl.program_id(1)

    @pl.when(sidx == 0)
    def _():
        ret_st[...] = jnp.zeros_like(ret_st)
        gla_st[...] = jnp.zeros_like(gla_st)
        hgrn_st[...] = jnp.zeros_like(hgrn_st)
        lru_h[...] = jnp.zeros_like(lru_h)
        lru_prev[...] = jnp.zeros_like(lru_prev)

    def col(i, width=g):
        return p_ref[0, :, i:i + width].astype(F32)

    avg_bf = avg_ref[...]
    row = lax.broadcasted_iota(jnp.int32, (sb, g), 0)

    c0 = 0
    rq, rk, rv, rg = col(c0), col(c0 + g), col(c0 + 2 * g), col(c0 + 3 * g)
    cos, sin = cos_ref[...], sin_ref[...]
    fh = fh_ref[...] > 0.5
    hd = g // HEADS

    def rope(t):
        swapped = jnp.where(fh, pltpu.roll(t, g - hd // 2, 1), pltpu.roll(t, hd // 2, 1))
        return t * cos + swapped * sin

    qr = rope(rq)
    kr = (rope(rk) * (hd ** -0.5))
    qr_bf, kr_bf, rv_bf = qr.astype(BF16), kr.astype(BF16), rv.astype(BF16)
    o = _dot_nt((qr * rqdec_ref[...]).astype(BF16), ret_st[...].astype(BF16))
    for hh in range(HEADS):
        sc = _dot_nt(qr_bf * hm64_ref[hh], kr_bf) * rmask_ref[hh]
        o += _dot(sc.astype(BF16), rv_bf * hm64_ref[hh])
    ret_st[...] = (ret_st[...] * rsdec_ref[...]
                   + _dot_tn(rv_bf, (kr * rkdec_ref[...]).astype(BF16)) * bd64_ref[...])
    y_ref[0, :, 0:g] = (_silu(rg) * _head_norm(o, avg_bf)).astype(y_ref.dtype)

    def lru_stages():
        c0 = 4 * g
        lx, lg = col(c0), col(c0 + g)
        prev = lru_prev[...]
        xc = lx * convw_ref[CONV_W - 1:CONV_W, :] + convb_ref[...]
        for sh in range(1, CONV_W):
            shifted = jnp.where(row >= sh, pltpu.roll(lx, sh, 0), pltpu.roll(prev, sh, 0))
            xc += shifted * convw_ref[CONV_W - 1 - sh:CONV_W - sh, :]
        lru_prev[...] = lx
        xc_bf = xc.astype(BF16)
        yield
        rgate = jax.nn.sigmoid(_dot(xc_bf, wa_ref[...]) + ba_ref[...])
        igate = jax.nn.sigmoid(_dot(xc_bf, wx_ref[...]) + bx_ref[...])
        log_a = (LRU_C * rgate) * _log_sigmoid(lam_ref[...])
        a = jnp.exp(log_a)
        u = jnp.sqrt(_neg_expm1(2.0 * log_a)) * (igate * xc)
        yield
        sh = 1
        while sh < sb:
            if sh % 8:
                keep = row >= sh
                a_sh = jnp.where(keep, pltpu.roll(a, sh, 0), 1.0)
                u_sh = jnp.where(keep, pltpu.roll(u, sh, 0), 0.0)
                u = u + a * u_sh
                a = a * a_sh
            else:
                u = jnp.concatenate([u[:sh], u[sh:] + a[sh:] * u[:sb - sh]], axis=0)
                a = jnp.concatenate([a[:sh], a[sh:] * a[:sb - sh]], axis=0)
            sh *= 2
            yield
        hseq = u + a * lru_h[...]
        lru_h[...] = hseq[sb - 1:sb, :]
        y_ref[0, :, g:2 * g] = (hseq * jax.nn.gelu(lg, approximate=True)).astype(y_ref.dtype)

    lru = lru_stages()

    c0 = 6 * g
    dk = g // 2
    gq = col(c0, dk) * ((dk // HEADS) ** -0.5)
    gk = col(c0 + dk, dk)
    gv = col(c0 + 2 * dk)
    gg = col(c0 + 2 * dk + g)
    ga = p_ref[0, :, 13 * g:13 * g + LANE]
    a_pre = _dot(ga, wa2_ref[...]) + bga_ref[...]
    logf = _log_sigmoid(a_pre) / GLA_TAU
    b_incl, b_after = _gated_scores(gq, gk, logf, seg_ref, maskn_ref, hm32_ref,
                                    b_ref, s_ref, sb, False, filler=lru)
    for _ in lru:
        pass
    gv_bf = gv.astype(BF16)
    o = _dot((gq * gk).astype(BF16), expg_ref[...]) * gv
    o += _dot_nt((gq * jnp.exp2(b_incl)).astype(BF16), gla_st[...].astype(BF16))
    for hh in range(HEADS):
        o += _dot(s_ref[hh].astype(BF16), gv_bf * hm64_ref[hh])
    gla_st[...] = (gla_st[...] * jnp.exp2(b_incl[sb - 1:sb, :])
                   + _dot_tn(gv_bf, (gk * jnp.exp2(b_after)).astype(BF16)) * bdgt_ref[...])
    y_ref[0, :, 2 * g:3 * g] = (_silu(gg) * _head_norm(o, avg_bf)).astype(y_ref.dtype)

    c0 = 9 * g
    hq, hf, hi_, hg = col(c0), col(c0 + g), col(c0 + 2 * g), col(c0 + 3 * g)
    logits = lbl_ref[...]
    ex = jnp.exp(logits - jnp.max(logits, axis=0, keepdims=True))
    sm = ex / jnp.sum(ex, axis=0, keepdims=True)
    lb = jnp.zeros((1, g), F32)
    for i in range(1, layer + 1):
        lb = lb + sm[i:i + 1, :]
    logf = _logaddexp(jnp.log(lb), jnp.log1p(-lb) + _log_sigmoid(hf))
    hk = (1.0 - lb) * jax.nn.sigmoid(-hf)
    hqs = _silu(hq)
    b_incl, b_after = _gated_scores(hqs, hk, logf, seg_ref, maskc_ref, hm64_ref,
                                    b_ref, s_ref, sb, True)
    hi_bf = hi_.astype(BF16)
    o = _dot((hqs * hk).astype(BF16), bd64_ref[...].astype(BF16)) * hi_
    o += _dot_nt((hqs * jnp.exp2(b_incl)).astype(BF16), hgrn_st[...].astype(BF16))
    for hh in range(HEADS):
        o += _dot(s_ref[hh].astype(BF16), hi_bf * hm64_ref[hh])
    hgrn_st[...] = (hgrn_st[...] * jnp.exp2(b_incl[sb - 1:sb, :])
                    + _dot_tn(hi_bf, (hk * jnp.exp2(b_after)).astype(BF16)) * bd64_ref[...])
    y_ref[0, :, 3 * g:4 * g] = (_silu(hg) * _head_norm(o, avg_bf)).astype(y_ref.dtype)


def _mixers(proj, cos_t, sin_t, consts, lw, *, layer, sb, group):
    b, s, n = proj.shape
    g = group

    def full(a):
        nd = a.ndim
        return pl.BlockSpec(a.shape, lambda i, j, nd=nd: (0,) * nd)

    const_args = [
        consts["seg_all"].astype(BF16), consts["mask_c"], consts["mask_n"],
        consts["avg"].astype(BF16), consts["hm64"].astype(BF16), consts["hm32"].astype(BF16),
        consts["bd64"],
        consts["bd_gla_t"], consts["expand_gla"].astype(BF16), consts["ret_mask"],
        consts["ret_qdec"], consts["ret_kdec"], consts["ret_sdec"], consts["first_half"],
    ]
    layer_args = [
        lw["conv_w"], lw["conv_b"], lw["wa_bd"], lw["ba"], lw["wx_bd"], lw["bx"], lw["lam"],
        lw["w_a2"], lw["b_a"], lw["lb_logits"],
    ]
    in_specs = ([pl.BlockSpec((1, sb, n), lambda i, j: (i, j, 0)),
                 pl.BlockSpec((sb, g), lambda i, j: (j, 0)),
                 pl.BlockSpec((sb, g), lambda i, j: (j, 0))]
                + [full(a) for a in const_args] + [full(a) for a in layer_args])
    return pl.pallas_call(
        functools.partial(_mixer_body, sb=sb, group=g, layer=layer),
        grid=(b, s // sb),
        in_specs=in_specs,
        out_specs=pl.BlockSpec((1, sb, N_GROUPS * g), lambda i, j: (i, j, 0)),
        out_shape=jax.ShapeDtypeStruct((b, s, N_GROUPS * g), BF16),
        scratch_shapes=[
            pltpu.VMEM((sb, g), F32),
            pltpu.VMEM((HEADS, sb, sb), F32),
            pltpu.VMEM((g, g), F32),
            pltpu.VMEM((g, g // 2), F32),
            pltpu.VMEM((g, g), F32),
            pltpu.VMEM((1, g), F32),
            pltpu.VMEM((sb, g), F32),
        ],
        compiler_params=pltpu.CompilerParams(
            dimension_semantics=("parallel", "arbitrary"), vmem_limit_bytes=VMEM_LIMIT),
        name="mixers",
    )(proj, cos_t, sin_t, *const_args, *layer_args)


def _block_diag(w):
    nb, bd, _ = w.shape
    out = jnp.zeros((nb * bd, nb * bd), w.dtype)
    for i in range(nb):
        out = out.at[i * bd:(i + 1) * bd, i * bd:(i + 1) * bd].set(w[i])
    return out


def _pad_cols(w, to):
    return jnp.pad(w, ((0, 0), (0, to - w.shape[1])))


def kernel(x, mem, ffn1_norm, ffn1_w_gu, ffn1_w_down, mix_norm, w_in, w_out, lru_conv_w, lru_conv_b, lru_wa, lru_ba, lru_wx, lru_bx, lru_lambda, gla_w_a2, gla_b_a, hgrn_lb_logits, xattn_norm, mem_norm, xattn_wq, xattn_wkv, xattn_wo, ffn2_norm, ffn2_w_gu, ffn2_w_down, final_norm):
    bsz, seq, d = x.shape
    depth = w_in.shape[0]
    n_mem = mem.shape[1]
    g = d // N_GROUPS
    sb = min(256, seq)
    t = bsz * seq

    hd = g // HEADS
    inv_freq = ROPE_BASE ** (-jnp.arange(hd // 2, dtype=F32) / (hd // 2))
    ang = jnp.arange(seq, dtype=F32)[:, None] * inv_freq[None, :]
    cos_t = jnp.tile(jnp.cos(ang), (1, 2 * HEADS))
    sin_t = jnp.tile(jnp.concatenate([-jnp.sin(ang), jnp.sin(ang)], axis=-1), (1, HEADS))

    consts = {k: jnp.asarray(v) for k, v in _mixer_consts(sb, g).items()}

    ga0 = 4 * g + 2 * g + g // 2 + g // 2 + g
    ga1 = ga0 + GLA_RANK
    bf = lambda a: a.astype(BF16)
    xf = x.reshape(t, d)
    memf = mem.reshape(bsz * n_mem, d)
    for l in range(depth):
        w_in_l = jnp.concatenate(
            [w_in[l][:, :ga0], w_in[l][:, ga1:], _pad_cols(w_in[l][:, ga0:ga1], LANE)], axis=1)
        lw = dict(
            conv_w=lru_conv_w[l], conv_b=lru_conv_b[l].reshape(1, g),
            wa_bd=bf(_block_diag(lru_wa[l])), ba=lru_ba[l].reshape(1, g),
            wx_bd=bf(_block_diag(lru_wx[l])), bx=lru_bx[l].reshape(1, g),
            lam=lru_lambda[l].reshape(1, g),
            w_a2=bf(jnp.pad(gla_w_a2[l], ((0, LANE - GLA_RANK), (0, 0)))),
            b_a=gla_b_a[l].reshape(1, g // 2),
            lb_logits=hgrn_lb_logits,
        )
        xf = _ffn(xf, ffn1_norm[l], bf(ffn1_w_gu[l]), bf(ffn1_w_down[l]))
        proj = _normproj(xf, mix_norm[l], bf(w_in_l), name="in_proj")
        y = _mixers(proj.reshape(bsz, seq, -1), cos_t, sin_t, consts, lw,
                    layer=l, sb=sb, group=g)
        kv = _normproj(memf, mem_norm[l], bf(xattn_wkv[l]), name="mem_kv")
        xf = _xattn(xf, y, kv.reshape(bsz, n_mem, 2 * d), bf(w_out[l]), xattn_norm[l],
                    bf(xattn_wq[l]), bf(xattn_wo[l]), seq=seq)
        xf = _ffn(xf, ffn2_norm[l], bf(ffn2_w_gu[l]), bf(ffn2_w_down[l]),
                  final_g=final_norm if l == depth - 1 else None)
    return xf.reshape(bsz, seq, d)
```

```python
import functools

import numpy as np
import jax
import jax.numpy as jnp
from jax import lax
from jax.experimental import pallas as pl
from jax.experimental.pallas import tpu as pltpu

F32 = jnp.float32
BF16 = jnp.bfloat16

CHUNK = 64
N_GROUPS = 4
HEADS = 4
ROPE_BASE = 10000.0
CONV_W = 4
LRU_C = 8.0
GLA_RANK = 16
GLA_TAU = 16.0
EPS = 1e-6
LANE = 128

VMEM_LIMIT = 56 * 1024 * 1024


def _rms(x, g):
    return x * lax.rsqrt(jnp.mean(x * x, axis=-1, keepdims=True) + EPS) * g


def _dot(a, b):
    return jnp.dot(a, b, preferred_element_type=F32)


def _dot_nt(a, b):
    return lax.dot_general(a, b, (((1,), (1,)), ((), ())), preferred_element_type=F32)


def _dot_tn(a, b):
    return lax.dot_general(a, b, (((0,), (0,)), ((), ())), preferred_element_type=F32)


def _split_hi_lo(x):
    hi = x.astype(BF16)
    lo = (x - hi.astype(F32)).astype(BF16)
    return hi, lo


def _sigmoid(x):
    return 1.0 / (1.0 + jnp.exp(-x))


def _log_sigmoid(x):
    return jnp.minimum(x, 0.0) - jnp.log(1.0 + jnp.exp(-jnp.abs(x)))


def _neg_expm1(x):
    t = jnp.tanh(0.5 * x)
    return -2.0 * t / (1.0 - t)


def _logaddexp(a, b):
    return jnp.maximum(a, b) + jnp.log(1.0 + jnp.exp(-jnp.abs(a - b)))


def _silu(x):
    return x * _sigmoid(x)


def _ffn_body(x_ref, xn_ref, g_ref, wg_ref, wu_ref, wd_ref, *rest, final, rows, la_rows):
    if final:
        gf_ref, o_ref, h_ref = rest
    else:
        o_ref, h_ref = rest
    i, j = pl.program_id(0), pl.program_id(1)
    tm = x_ref.shape[0]
    n_la = tm // la_rows
    slot = i % 2

    @pl.when(jnp.logical_and(i == 0, j == 0))
    def _():
        h_ref[0] = _rms(x_ref[...], g_ref[...]).astype(BF16)

    def step(first, lookahead):
        for r0 in range(0, tm, rows):
            h = h_ref[slot, r0:r0 + rows, :]
            gate = _dot(h, wg_ref[...])
            up = _dot(h, wu_ref[...])
            act = (0.5 * _silu(gate) * up).astype(BF16)
            down = _dot(act, wd_ref[...])
            if first:
                o_ref[r0:r0 + rows, :] = x_ref[r0:r0 + rows, :] + down
            else:
                o_ref[r0:r0 + rows, :] += down
        if lookahead:
            off = pl.multiple_of(j * la_rows, la_rows)
            h_ref[1 - slot, pl.ds(off, la_rows), :] = (
                _rms(xn_ref[...], g_ref[...]).astype(BF16))

    pl.when(j == 0)(functools.partial(step, True, True))
    pl.when(jnp.logical_and(j > 0, j < n_la))(functools.partial(step, False, True))
    pl.when(j >= n_la)(functools.partial(step, False, False))

    if final:
        @pl.when(j == pl.num_programs(1) - 1)
        def _():
            o_ref[...] = _rms(o_ref[...], gf_ref[...])


def _ffn(x, g, w_gu, w_down, final_g=None, *, tm=2048, tf=256, rows=1024, la_rows=256):
    t, d = x.shape
    f = w_down.shape[0]
    nj = f // tf
    tm = min(tm, t)
    rows = min(rows, tm)
    nt = t // tm
    n_la = tm // la_rows
    assert n_la <= nj and tm % rows == 0

    def next_slab(i, j):
        return (jnp.minimum(i + 1, nt - 1) * n_la + jnp.minimum(j, n_la - 1), 0)

    in_specs = [
        pl.BlockSpec((tm, d), lambda i, j: (i, 0)),
        pl.BlockSpec((la_rows, d), next_slab),
        pl.BlockSpec((1, d), lambda i, j: (0, 0)),
        pl.BlockSpec((d, tf), lambda i, j: (0, j)),
        pl.BlockSpec((d, tf), lambda i, j: (0, j + nj)),
        pl.BlockSpec((tf, d), lambda i, j: (j, 0)),
    ]
    args = [x, x, g.reshape(1, d), w_gu, w_gu, w_down]
    if final_g is not None:
        in_specs.append(pl.BlockSpec((1, d), lambda i, j: (0, 0)))
        args.append(final_g.reshape(1, d))
    return pl.pallas_call(
        functools.partial(_ffn_body, final=final_g is not None, rows=rows, la_rows=la_rows),
        grid=(nt, nj),
        in_specs=in_specs,
        out_specs=pl.BlockSpec((tm, d), lambda i, j: (i, 0)),
        out_shape=jax.ShapeDtypeStruct((t, d), F32),
        scratch_shapes=[pltpu.VMEM((2, tm, d), BF16)],
        compiler_params=pltpu.CompilerParams(
            dimension_semantics=("arbitrary", "arbitrary"), vmem_limit_bytes=VMEM_LIMIT),
        name="ffn_final" if final_g is not None else "ffn",
    )(*args)


def _normproj_body(x_ref, g_ref, w_ref, o_ref):
    h = _rms(x_ref[...], g_ref[...]).astype(BF16)
    o_ref[...] = _dot(h, w_ref[...]).astype(o_ref.dtype)


def _normproj(x, g, w, *, tm=512, name="normproj"):
    t, d = x.shape
    n = w.shape[1]
    return pl.pallas_call(
        _normproj_body,
        grid=(t // tm,),
        in_specs=[
            pl.BlockSpec((tm, d), lambda i: (i, 0)),
            pl.BlockSpec((1, d), lambda i: (0, 0)),
            pl.BlockSpec((d, n), lambda i: (0, 0)),
        ],
        out_specs=pl.BlockSpec((tm, n), lambda i: (i, 0)),
        out_shape=jax.ShapeDtypeStruct((t, n), BF16),
        compiler_params=pltpu.CompilerParams(
            dimension_semantics=("parallel",), vmem_limit_bytes=VMEM_LIMIT),
        name=name,
    )(x, g.reshape(1, d), w)


def _xattn_body(x_ref, y_ref, kv_ref, wout_ref, g_ref, wq_ref, wo_ref, o_ref):
    d = x_ref.shape[-1]
    hd = d // HEADS
    x1 = x_ref[...] + _dot(y_ref[0], wout_ref[...])
    h = _rms(x1, g_ref[...]).astype(BF16)
    q = _dot(h, wq_ref[...]).astype(BF16)
    kv = kv_ref[0]
    outs = []
    for hh in range(HEADS):
        qh = q[:, hh * hd:(hh + 1) * hd]
        kh = kv[:, hh * hd:(hh + 1) * hd]
        vh = kv[:, d + hh * hd:d + (hh + 1) * hd]
        s = _dot_nt(qh, kh) * (hd ** -0.5)
        s = s - jnp.max(s, axis=-1, keepdims=True)
        e = jnp.exp(s)
        p = e / jnp.sum(e, axis=-1, keepdims=True)
        outs.append(_dot(p.astype(BF16), vh).astype(BF16))
    o = jnp.concatenate(outs, axis=-1)
    o_ref[...] = x1 + _dot(o, wo_ref[...])


def _xattn(x, y, kv, w_out, g, wq, wo, *, seq, tm=512):
    t, d = x.shape
    b = t // seq
    ns = seq // tm
    n_mem = kv.shape[1]
    return pl.pallas_call(
        _xattn_body,
        grid=(b, ns),
        in_specs=[
            pl.BlockSpec((tm, d), lambda i, j: (i * ns + j, 0)),
            pl.BlockSpec((1, tm, d), lambda i, j: (i, j, 0)),
            pl.BlockSpec((1, n_mem, 2 * d), lambda i, j: (i, 0, 0)),
            pl.BlockSpec((d, d), lambda i, j: (0, 0)),
            pl.BlockSpec((1, d), lambda i, j: (0, 0)),
            pl.BlockSpec((d, d), lambda i, j: (0, 0)),
            pl.BlockSpec((d, d), lambda i, j: (0, 0)),
        ],
        out_specs=pl.BlockSpec((tm, d), lambda i, j: (i * ns + j, 0)),
        out_shape=jax.ShapeDtypeStruct((t, d), F32),
        compiler_params=pltpu.CompilerParams(
            dimension_semantics=("parallel", "parallel"), vmem_limit_bytes=VMEM_LIMIT),
        name="outproj_xattn",
    )(x, y, kv, w_out, g.reshape(1, d), wq, wo)


LOG2_E = 1.4426950408889634
SEG_WIDTHS = (2,)


def _level_widths(sb):
    ws, w = [], sb // 2
    while w >= 1:
        ws.append(w)
        w //= 2
    return ws


def _mixer_consts(sb, group):
    r = np.arange(sb)
    ws = _level_widths(sb)
    seg, mask_c, mask_n = [], [], []
    for w in ws:
        start = (r // (2 * w)) * (2 * w)
        split = start + w
        upper = r >= split
        m = np.zeros((sb, sb), np.float32)
        for j in range(sb):
            if upper[j]:
                m[j, split[j]:j + 1] = 1.0
            else:
                m[j, j + 1:split[j]] = 1.0
        seg.append(m)
        same = start[:, None] == start[None, :]
        causal = same & upper[:, None] & ~upper[None, :]
        anti = same & ~upper[:, None] & upper[None, :]
        mask_c.append(causal.astype(np.float32))
        mask_n.append((causal | (anti & (2 * w <= CHUNK))).astype(np.float32))
    tri_incl = (r[:, None] >= r[None, :]).astype(np.float32)
    seg_all = np.concatenate([seg[ws.index(w)] for w in SEG_WIDTHS] + [tri_incl], axis=0)

    hd = group // HEADS
    ch = np.arange(group)
    head_of = ch // hd
    avg = (head_of[:, None] == head_of[None, :]).astype(np.float32) / hd
    hm64 = (head_of[None, :] == np.arange(HEADS)[:, None]).astype(np.float32)[:, None, :]
    ck = np.arange(group // 2)
    head_of_k = ck // (hd // 2)
    hm32 = (head_of_k[None, :] == np.arange(HEADS)[:, None]).astype(np.float32)[:, None, :]
    bd_gla_t = (head_of[:, None] == head_of_k[None, :]).astype(np.float32)
    expand_gla = bd_gla_t.T.copy()

    log_gamma = np.log1p(-np.exp2(-5.0 - np.arange(HEADS, dtype=np.float64)))
    cj, cm = r[:, None] // CHUNK, r[None, :] // CHUNK
    dist = np.where(cj == cm, np.abs(r[:, None] - r[None, :]), r[:, None] - r[None, :])
    ret_mask = np.stack([np.where(cm <= cj, np.exp(dist * lg), 0.0) for lg in log_gamma])
    lg_ch = log_gamma[head_of]
    ret_qdec = np.exp((r[:, None] + 1) * lg_ch[None, :])
    ret_kdec = np.exp((sb - 1 - r[:, None]) * lg_ch[None, :])
    ret_sdec = np.exp(sb * lg_ch)[None, :]
    first_half = ((ch % hd) < hd // 2).astype(np.float32)[None, :]
    shift_all = np.concatenate(
        [(r[:, None] - k == r[None, :]).astype(np.float32) for k in range(1, CONV_W)], axis=0)
    return dict(
        seg_all=seg_all, mask_c=np.stack(mask_c), mask_n=np.stack(mask_n),
        avg=avg, hm64=hm64, hm32=hm32, bd64=(avg > 0).astype(np.float32),
        bd_gla_t=bd_gla_t, expand_gla=expand_gla,
        ret_mask=ret_mask.astype(np.float32), ret_qdec=ret_qdec.astype(np.float32),
        ret_kdec=ret_kdec.astype(np.float32), ret_sdec=ret_sdec.astype(np.float32),
        first_half=first_half, shift_all=shift_all,
    )


def _head_norm(o, avg_bf):
    mu = _dot(o.astype(BF16), avg_bf)
    dlt = o - mu
    var = _dot((dlt * dlt).astype(BF16), avg_bf)
    return dlt * lax.rsqrt(var + EPS)


def _gated_stages(q, k, v_bf, logf, seg_ref, mask_ref, hmq_ref, hmv_ref, b_ref, s_ref, st_ref,
                  diag_ref, bd_ref, sb, causal_in_chunk, out):
    dk = logf.shape[1]
    half = sb // 2
    logf = logf * LOG2_E
    hi, lo = _split_hi_lo(logf)
    both = _dot(seg_ref[...], jnp.concatenate([hi, lo], axis=1))
    sums = both[:, :dk] + both[:, dk:]
    fine = {w: sums[i * sb:(i + 1) * sb] for i, w in enumerate(SEG_WIDTHS)}
    b_incl = sums[len(SEG_WIDTHS) * sb:]
    b_ref[:, :dk] = b_incl
    odd = lax.broadcasted_iota(jnp.int32, (sb, dk), 0) % 2 == 1
    s_ref[...] = jnp.zeros_like(s_ref)
    yield
    for lvl, w in enumerate(_level_widths(sb)):
        if w in fine:
            d = fine[w]
        elif w == 1:
            d = jnp.where(odd, logf, 0.0)
        else:
            beta = jnp.concatenate(
                [jnp.broadcast_to(b_ref[gs + w - 1:gs + w, :dk], (2 * w, dk))
                 for gs in range(0, sb, 2 * w)], axis=0)
            d = -jnp.abs(b_incl - beta)
        e = jnp.exp2(d)
        ks = (k * e).astype(BF16)
        causal = causal_in_chunk or 2 * w > CHUNK
        if causal and w >= 8:
            runs = list(range(0, sb, 2 * w))
            take = lambda t: jnp.concatenate([t[gs + w:gs + 2 * w] for gs in runs], axis=0)
            qs = (take(q) * take(e)).astype(BF16)
            for pair in range(HEADS // 2):
                lhs = jnp.concatenate([qs * hmq_ref[2 * pair], qs * hmq_ref[2 * pair + 1]], axis=0)
                sc = _dot_nt(lhs, ks)
                for sub in range(2):
                    hh = 2 * pair + sub
                    for gi, gs in enumerate(runs):
                        c0 = (gs // LANE) * LANE
                        c1 = c0 + min(LANE, sb)
                        part = sc[sub * half + gi * w:sub * half + (gi + 1) * w, c0:c1]
                        if 2 * w < sb or w % LANE:
                            part = part * mask_ref[lvl, gs + w:gs + 2 * w, c0:c1]
                        s_ref[hh, gs + w:gs + 2 * w, c0:c1] += part
        else:
            qs = (q * e).astype(BF16)
            for hh in range(HEADS):
                sc = _dot_nt(qs * hmq_ref[hh], ks)
                for t in range(0, sb, LANE):
                    s_ref[hh, t:t + LANE, t:t + LANE] += (
                        sc[t:t + LANE, t:t + LANE] * mask_ref[lvl, t:t + LANE, t:t + LANE])
        yield
    b_after = b_ref[sb - 1:sb, :dk] - b_incl
    o = _dot((q * k).astype(BF16), diag_ref[...].astype(BF16)) * v_bf.astype(F32)
    o += _dot_nt((q * jnp.exp2(b_incl)).astype(BF16), st_ref[...].astype(BF16))
    yield
    for hh in range(HEADS):
        o += _dot(s_ref[hh].astype(BF16), v_bf * hmv_ref[hh])
    yield
    st_ref[...] = (st_ref[...] * jnp.exp2(b_incl[sb - 1:sb, :])
                   + _dot_tn(v_bf, (k * jnp.exp2(b_after)).astype(BF16)) * bd_ref[...])
    out.append(o)


def _mixer_body(p_ref, cos_ref, sin_ref,
                seg_ref, maskc_ref, maskn_ref, avg_ref, hm64_ref, hm32_ref, bd64_ref,
                bdgt_ref, expg_ref, rmask_ref, rqdec_ref, rkdec_ref, rsdec_ref, fh_ref, shift_ref,
                convw_ref, convb_ref, wa_ref, ba_ref, wx_ref, bx_ref, lam_ref,
                wa2_ref, bga_ref, lbl_ref,
                y_ref,
                bg_ref, bh_ref, sg_ref, sh_ref, ret_st, gla_st, hgrn_st, lru_h, lru_prev,
                *, sb, group, layer):
    g = group
    sidx = pl.program_id(1)

    @pl.when(sidx == 0)
    def _():
        ret_st[...] = jnp.zeros_like(ret_st)
        gla_st[...] = jnp.zeros_like(gla_st)
        hgrn_st[...] = jnp.zeros_like(hgrn_st)
        lru_h[...] = jnp.zeros_like(lru_h)
        lru_prev[...] = jnp.zeros_like(lru_prev)

    def col(i, width=g):
        return p_ref[0, :, i:i + width].astype(F32)

    avg_bf = avg_ref[...]
    row = lax.broadcasted_iota(jnp.int32, (sb, g), 0)

    def finish(o, gate, c0):
        y_ref[0, :, c0:c0 + g] = (_silu(gate) * _head_norm(o, avg_bf)).astype(y_ref.dtype)


    def retention():
        rq, rk, rv = col(0), col(g), p_ref[0, :, 2 * g:3 * g]
        cos, sin = cos_ref[...], sin_ref[...]
        fh = fh_ref[...] > 0.5
        hd = g // HEADS

        def rope(t):
            swapped = jnp.where(fh, pltpu.roll(t, g - hd // 2, 1), pltpu.roll(t, hd // 2, 1))
            return t * cos + swapped * sin

        qr = rope(rq)
        kr = (rope(rk) * (hd ** -0.5))
        qr_bf, kr_bf = qr.astype(BF16), kr.astype(BF16)
        yield
        o = _dot_nt((qr * rqdec_ref[...]).astype(BF16), ret_st[...].astype(BF16))
        for hh in range(HEADS):
            sc = _dot_nt(qr_bf * hm64_ref[hh], kr_bf) * rmask_ref[hh]
            o += _dot(sc.astype(BF16), rv * hm64_ref[hh])
            yield
        ret_st[...] = (ret_st[...] * rsdec_ref[...]
                       + _dot_tn(rv, (kr * rkdec_ref[...]).astype(BF16)) * bd64_ref[...])
        yield
        finish(o, col(3 * g), 0)

    def rglru():
        c0 = 4 * g
        lx_bf, lg = p_ref[0, :, c0:c0 + g], col(c0 + g)
        lx = lx_bf.astype(F32)
        shifted = _dot(shift_ref[...], lx_bf)
        tail = lru_prev[...]
        row8 = lax.broadcasted_iota(jnp.int32, tail.shape, 0)
        xc = lx * convw_ref[CONV_W - 1:CONV_W, :] + convb_ref[...]
        fix = jnp.zeros_like(tail)
        for sh in range(1, CONV_W):
            tap = convw_ref[CONV_W - 1 - sh:CONV_W - sh, :]
            xc += shifted[(sh - 1) * sb:sh * sb] * tap
            fix += jnp.where(row8 < sh, pltpu.roll(tail, sh, 0), 0.0) * tap
        xc = jnp.concatenate([xc[:8] + fix, xc[8:]], axis=0)
        lru_prev[...] = lx[sb - 8:sb]
        xc_bf = xc.astype(BF16)
        yield
        rgate = _sigmoid(_dot(xc_bf, wa_ref[...]) + ba_ref[...])
        igate = _sigmoid(_dot(xc_bf, wx_ref[...]) + bx_ref[...])
        log_a = (LRU_C * rgate) * _log_sigmoid(lam_ref[...])
        a = jnp.exp(log_a)
        u = jnp.sqrt(_neg_expm1(2.0 * log_a)) * (igate * xc)
        yield
        sh = 1
        while sh < sb:
            if sh % 8:
                keep = row >= sh
                a_sh = jnp.where(keep, pltpu.roll(a, sh, 0), 1.0)
                u_sh = jnp.where(keep, pltpu.roll(u, sh, 0), 0.0)
                u = u + a * u_sh
                a = a * a_sh
            else:
                u = jnp.concatenate([u[:sh], u[sh:] + a[sh:] * u[:sb - sh]], axis=0)
                a = jnp.concatenate([a[:sh], a[sh:] * a[:sb - sh]], axis=0)
            sh *= 2
            yield
        hseq = u + a * lru_h[...]
        lru_h[...] = hseq[sb - 1:sb, :]
        y_ref[0, :, g:2 * g] = (hseq * jax.nn.gelu(lg, approximate=True)).astype(y_ref.dtype)

    def gla():
        c0 = 6 * g
        dk = g // 2
        gq = col(c0, dk) * ((dk // HEADS) ** -0.5)
        gk = col(c0 + dk, dk)
        gv = p_ref[0, :, c0 + 2 * dk:c0 + 2 * dk + g]
        ga = p_ref[0, :, 13 * g:13 * g + LANE]
        a_pre = _dot(ga, wa2_ref[...]) + bga_ref[...]
        logf = _log_sigmoid(a_pre) / GLA_TAU
        out = []
        yield from _gated_stages(gq, gk, gv, logf, seg_ref, maskn_ref, hm32_ref, hm64_ref,
                                 bg_ref, sg_ref, gla_st, expg_ref, bdgt_ref, sb, False, out)
        yield
        finish(out[0], col(c0 + 2 * dk + g), 2 * g)

    def hgrn2():
        c0 = 9 * g
        hq, hf, hv = col(c0), col(c0 + g), p_ref[0, :, c0 + 2 * g:c0 + 3 * g]
        logits = lbl_ref[...]
        ex = jnp.exp(logits - jnp.max(logits, axis=0, keepdims=True))
        sm = ex / jnp.sum(ex, axis=0, keepdims=True)
        lb = jnp.zeros((1, g), F32)
        for i in range(1, layer + 1):
            lb = lb + sm[i:i + 1, :]
        logf = _logaddexp(jnp.log(lb), jnp.log1p(-lb) + _log_sigmoid(hf))
        hk = (1.0 - lb) * _sigmoid(-hf)
        hqs = _silu(hq)
        out = []
        yield from _gated_stages(hqs, hk, hv, logf, seg_ref, maskc_ref, hm64_ref, hm64_ref,
                                 bh_ref, sh_ref, hgrn_st, bd64_ref, bd64_ref, sb, True, out)
        yield
        finish(out[0], col(c0 + 3 * g), 3 * g)

    waiting = [[hgrn2(), gla()], [retention()], [rglru()]]
    live = []
    while live or waiting:
        if waiting:
            live += waiting.pop(0)
        live = [m for m in live if next(m, StopIteration) is not StopIteration]


def _mixers(proj, cos_t, sin_t, consts, lw, *, layer, sb, group):
    b, s, n = proj.shape
    g = group

    def full(a):
        nd = a.ndim
        return pl.BlockSpec(a.shape, lambda i, j, nd=nd: (0,) * nd)

    const_args = [
        consts["seg_all"].astype(BF16), consts["mask_c"], consts["mask_n"],
        consts["avg"].astype(BF16), consts["hm64"].astype(BF16), consts["hm32"].astype(BF16),
        consts["bd64"],
        consts["bd_gla_t"], consts["expand_gla"].astype(BF16), consts["ret_mask"],
        consts["ret_qdec"], consts["ret_kdec"], consts["ret_sdec"], consts["first_half"],
        consts["shift_all"].astype(BF16),
    ]
    layer_args = [
        lw["conv_w"], lw["conv_b"], lw["wa_bd"], lw["ba"], lw["wx_bd"], lw["bx"], lw["lam"],
        lw["w_a2"], lw["b_a"], lw["lb_logits"],
    ]
    in_specs = ([pl.BlockSpec((1, sb, n), lambda i, j: (i, j, 0)),
                 pl.BlockSpec((sb, g), lambda i, j: (j, 0)),
                 pl.BlockSpec((sb, g), lambda i, j: (j, 0))]
                + [full(a) for a in const_args] + [full(a) for a in layer_args])
    return pl.pallas_call(
        functools.partial(_mixer_body, sb=sb, group=g, layer=layer),
        grid=(b, s // sb),
        in_specs=in_specs,
        out_specs=pl.BlockSpec((1, sb, N_GROUPS * g), lambda i, j: (i, j, 0)),
        out_shape=jax.ShapeDtypeStruct((b, s, N_GROUPS * g), BF16),
        scratch_shapes=[
            pltpu.VMEM((sb, g), F32),
            pltpu.VMEM((sb, g), F32),
            pltpu.VMEM((HEADS, sb, sb), F32),
            pltpu.VMEM((HEADS, sb, sb), F32),
            pltpu.VMEM((g, g), F32),
            pltpu.VMEM((g, g // 2), F32),
            pltpu.VMEM((g, g), F32),
            pltpu.VMEM((1, g), F32),
            pltpu.VMEM((8, g), F32),
        ],
        compiler_params=pltpu.CompilerParams(
            dimension_semantics=("parallel", "arbitrary"), vmem_limit_bytes=VMEM_LIMIT),
        name="mixers",
    )(proj, cos_t, sin_t, *const_args, *layer_args)


def _block_diag(w):
    nb, bd, _ = w.shape
    out = jnp.zeros((nb * bd, nb * bd), w.dtype)
    for i in range(nb):
        out = out.at[i * bd:(i + 1) * bd, i * bd:(i + 1) * bd].set(w[i])
    return out


def _pad_cols(w, to):
    return jnp.pad(w, ((0, 0), (0, to - w.shape[1])))


def kernel(x, mem, ffn1_norm, ffn1_w_gu, ffn1_w_down, mix_norm, w_in, w_out, lru_conv_w, lru_conv_b, lru_wa, lru_ba, lru_wx, lru_bx, lru_lambda, gla_w_a2, gla_b_a, hgrn_lb_logits, xattn_norm, mem_norm, xattn_wq, xattn_wkv, xattn_wo, ffn2_norm, ffn2_w_gu, ffn2_w_down, final_norm):
    bsz, seq, d = x.shape
    depth = w_in.shape[0]
    n_mem = mem.shape[1]
    g = d // N_GROUPS
    sb = min(256, seq)
    t = bsz * seq

    hd = g // HEADS
    inv_freq = ROPE_BASE ** (-jnp.arange(hd // 2, dtype=F32) / (hd // 2))
    ang = jnp.arange(seq, dtype=F32)[:, None] * inv_freq[None, :]
    cos_t = jnp.tile(jnp.cos(ang), (1, 2 * HEADS))
    sin_t = jnp.tile(jnp.concatenate([-jnp.sin(ang), jnp.sin(ang)], axis=-1), (1, HEADS))

    consts = {k: jnp.asarray(v) for k, v in _mixer_consts(sb, g).items()}

    ga0 = 4 * g + 2 * g + g // 2 + g // 2 + g
    ga1 = ga0 + GLA_RANK
    bf = lambda a: a.astype(BF16)
    xf = x.reshape(t, d)
    memf = mem.reshape(bsz * n_mem, d)
    for l in range(depth):
        w_in_l = jnp.concatenate(
            [w_in[l][:, :ga0], w_in[l][:, ga1:], _pad_cols(w_in[l][:, ga0:ga1], LANE)], axis=1)
        lw = dict(
            conv_w=lru_conv_w[l], conv_b=lru_conv_b[l].reshape(1, g),
            wa_bd=bf(_block_diag(lru_wa[l])), ba=lru_ba[l].reshape(1, g),
            wx_bd=bf(_block_diag(lru_wx[l])), bx=lru_bx[l].reshape(1, g),
            lam=lru_lambda[l].reshape(1, g),
            w_a2=bf(jnp.pad(gla_w_a2[l], ((0, LANE - GLA_RANK), (0, 0)))),
            b_a=gla_b_a[l].reshape(1, g // 2),
            lb_logits=hgrn_lb_logits,
        )
        xf = _ffn(xf, ffn1_norm[l], bf(ffn1_w_gu[l]), bf(ffn1_w_down[l]))
        proj = _normproj(xf, mix_norm[l], bf(w_in_l), name="in_proj")
        y = _mixers(proj.reshape(bsz, seq, -1), cos_t, sin_t, consts, lw,
                    layer=l, sb=sb, group=g)
        kv = _normproj(memf, mem_norm[l], bf(xattn_wkv[l]), name="mem_kv")
        xf = _xattn(xf, y, kv.reshape(bsz, n_mem, 2 * d), bf(w_out[l]), xattn_norm[l],
                    bf(xattn_wq[l]), bf(xattn_wo[l]), seq=seq)
        xf = _ffn(xf, ffn2_norm[l], bf(ffn2_w_gu[l]), bf(ffn2_w_down[l]),
                  final_g=final_norm if l == depth - 1 else None)
    return xf.reshape(bsz, seq, d)
```

```python
import functools

import numpy as np
import jax
import jax.numpy as jnp
from jax import lax
from jax.experimental import pallas as pl
from jax.experimental.pallas import tpu as pltpu

F32 = jnp.float32
BF16 = jnp.bfloat16

CHUNK = 64
N_GROUPS = 4
HEADS = 4
ROPE_BASE = 10000.0
CONV_W = 4
LRU_C = 8.0
GLA_RANK = 16
GLA_TAU = 16.0
EPS = 1e-6
LANE = 128

VMEM_LIMIT = 56 * 1024 * 1024


def _rms(x, g):
    return x * lax.rsqrt(jnp.mean(x * x, axis=-1, keepdims=True) + EPS) * g


def _dot(a, b):
    return jnp.dot(a, b, preferred_element_type=F32)


def _dot_nt(a, b):
    return lax.dot_general(a, b, (((1,), (1,)), ((), ())), preferred_element_type=F32)


def _dot_tn(a, b):
    return lax.dot_general(a, b, (((0,), (0,)), ((), ())), preferred_element_type=F32)


def _split_hi_lo(x):
    hi = x.astype(BF16)
    lo = (x - hi.astype(F32)).astype(BF16)
    return hi, lo


def _sigmoid(x):
    return 1.0 / (1.0 + jnp.exp(-x))


def _log_sigmoid(x):
    return jnp.minimum(x, 0.0) - jnp.log(1.0 + jnp.exp(-jnp.abs(x)))


def _neg_expm1(x):
    t = jnp.tanh(0.5 * x)
    return -2.0 * t / (1.0 - t)


def _logaddexp(a, b):
    return jnp.maximum(a, b) + jnp.log(1.0 + jnp.exp(-jnp.abs(a - b)))


def _silu(x):
    return x * _sigmoid(x)


def _ffn_body(x_ref, xn_ref, g_ref, wg_ref, wu_ref, wd_ref, *rest, final, rows, la_rows):
    if final:
        gf_ref, o_ref, h_ref = rest
    else:
        o_ref, h_ref = rest
    i, j = pl.program_id(0), pl.program_id(1)
    tm = x_ref.shape[0]
    n_la = tm // la_rows
    slot = i % 2

    @pl.when(jnp.logical_and(i == 0, j == 0))
    def _():
        h_ref[0] = _rms(x_ref[...], g_ref[...]).astype(BF16)

    def step(first, lookahead):
        for r0 in range(0, tm, rows):
            h = h_ref[slot, r0:r0 + rows, :]
            gate = _dot(h, wg_ref[...])
            up = _dot(h, wu_ref[...])
            act = (0.5 * _silu(gate) * up).astype(BF16)
            down = _dot(act, wd_ref[...])
            if first:
                o_ref[r0:r0 + rows, :] = x_ref[r0:r0 + rows, :] + down
            else:
                o_ref[r0:r0 + rows, :] += down
        if lookahead:
            off = pl.multiple_of(j * la_rows, la_rows)
            h_ref[1 - slot, pl.ds(off, la_rows), :] = (
                _rms(xn_ref[...], g_ref[...]).astype(BF16))

    pl.when(j == 0)(functools.partial(step, True, True))
    pl.when(jnp.logical_and(j > 0, j < n_la))(functools.partial(step, False, True))
    pl.when(j >= n_la)(functools.partial(step, False, False))

    if final:
        @pl.when(j == pl.num_programs(1) - 1)
        def _():
            o_ref[...] = _rms(o_ref[...], gf_ref[...])


def _ffn(x, g, w_gu, w_down, layer, final_g=None, *, tm=2048, tf=256, rows=1024, la_rows=256):
    t, d = x.shape
    f = w_down.shape[1]
    nj = f // tf
    tm = min(tm, t)
    rows = min(rows, tm)
    nt = t // tm
    n_la = tm // la_rows
    assert n_la <= nj and tm % rows == 0

    def next_slab(i, j):
        return (jnp.minimum(i + 1, nt - 1) * n_la + jnp.minimum(j, n_la - 1), 0)

    in_specs = [
        pl.BlockSpec((tm, d), lambda i, j: (i, 0)),
        pl.BlockSpec((la_rows, d), next_slab),
        pl.BlockSpec((1, d), lambda i, j: (0, 0)),
        pl.BlockSpec((None, d, tf), lambda i, j: (layer, 0, j)),
        pl.BlockSpec((None, d, tf), lambda i, j: (layer, 0, j + nj)),
        pl.BlockSpec((None, tf, d), lambda i, j: (layer, j, 0)),
    ]
    args = [x, x, g.reshape(1, d), w_gu, w_gu, w_down]
    if final_g is not None:
        in_specs.append(pl.BlockSpec((1, d), lambda i, j: (0, 0)))
        args.append(final_g.reshape(1, d))
    return pl.pallas_call(
        functools.partial(_ffn_body, final=final_g is not None, rows=rows, la_rows=la_rows),
        grid=(nt, nj),
        in_specs=in_specs,
        out_specs=pl.BlockSpec((tm, d), lambda i, j: (i, 0)),
        out_shape=jax.ShapeDtypeStruct((t, d), F32),
        scratch_shapes=[pltpu.VMEM((2, tm, d), BF16)],
        compiler_params=pltpu.CompilerParams(
            dimension_semantics=("arbitrary", "arbitrary"), vmem_limit_bytes=VMEM_LIMIT),
        name="ffn_final" if final_g is not None else "ffn",
    )(*args)


def _normproj_body(x_ref, g_ref, w_ref, o_ref):
    h = _rms(x_ref[...], g_ref[...]).astype(BF16)
    o_ref[...] = _dot(h, w_ref[...]).astype(o_ref.dtype)


def _normproj(x, g, w, layer, *, tm=512, name="normproj"):
    t, d = x.shape
    n = w.shape[2]
    return pl.pallas_call(
        _normproj_body,
        grid=(t // tm,),
        in_specs=[
            pl.BlockSpec((tm, d), lambda i: (i, 0)),
            pl.BlockSpec((1, d), lambda i: (0, 0)),
            pl.BlockSpec((None, d, n), lambda i: (layer, 0, 0)),
        ],
        out_specs=pl.BlockSpec((tm, n), lambda i: (i, 0)),
        out_shape=jax.ShapeDtypeStruct((t, n), BF16),
        compiler_params=pltpu.CompilerParams(
            dimension_semantics=("parallel",), vmem_limit_bytes=VMEM_LIMIT),
        name=name,
    )(x, g.reshape(1, d), w)


def _xattn_body(x_ref, y_ref, kv_ref, wout_ref, g_ref, wq_ref, wo_ref, o_ref):
    d = x_ref.shape[-1]
    hd = d // HEADS
    x1 = x_ref[...] + _dot(y_ref[0], wout_ref[...])
    h = _rms(x1, g_ref[...]).astype(BF16)
    q = _dot(h, wq_ref[...]).astype(BF16)
    kv = kv_ref[0]
    outs = []
    for hh in range(HEADS):
        qh = q[:, hh * hd:(hh + 1) * hd]
        kh = kv[:, hh * hd:(hh + 1) * hd]
        vh = kv[:, d + hh * hd:d + (hh + 1) * hd]
        s = _dot_nt(qh, kh) * (hd ** -0.5)
        s = s - jnp.max(s, axis=-1, keepdims=True)
        e = jnp.exp(s)
        p = e / jnp.sum(e, axis=-1, keepdims=True)
        outs.append(_dot(p.astype(BF16), vh).astype(BF16))
    o = jnp.concatenate(outs, axis=-1)
    o_ref[...] = x1 + _dot(o, wo_ref[...])


def _xattn(x, y, kv, w_out, g, wq, wo, layer, *, seq, tm=1024):
    t, d = x.shape
    b = t // seq
    tm = min(tm, seq)
    ns = seq // tm
    n_mem = kv.shape[1]
    wspec = pl.BlockSpec((None, d, d), lambda i, j: (layer, 0, 0))
    return pl.pallas_call(
        _xattn_body,
        grid=(b, ns),
        in_specs=[
            pl.BlockSpec((tm, d), lambda i, j: (i * ns + j, 0)),
            pl.BlockSpec((1, tm, d), lambda i, j: (i, j, 0)),
            pl.BlockSpec((1, n_mem, 2 * d), lambda i, j: (i, 0, 0)),
            wspec,
            pl.BlockSpec((1, d), lambda i, j: (0, 0)),
            wspec,
            wspec,
        ],
        out_specs=pl.BlockSpec((tm, d), lambda i, j: (i * ns + j, 0)),
        out_shape=jax.ShapeDtypeStruct((t, d), F32),
        compiler_params=pltpu.CompilerParams(
            dimension_semantics=("parallel", "parallel"), vmem_limit_bytes=VMEM_LIMIT),
        name="outproj_xattn",
    )(x, y, kv, w_out, g.reshape(1, d), wq, wo)


LOG2_E = 1.4426950408889634
SEG_WIDTHS = (2,)


def _level_widths(sb):
    ws, w = [], sb // 2
    while w >= 1:
        ws.append(w)
        w //= 2
    return ws


def _mixer_consts(sb, group):
    r = np.arange(sb)
    ws = _level_widths(sb)
    seg, mask_c, mask_n = [], [], []
    for w in ws:
        start = (r // (2 * w)) * (2 * w)
        split = start + w
        upper = r >= split
        m = np.zeros((sb, sb), np.float32)
        for j in range(sb):
            if upper[j]:
                m[j, split[j]:j + 1] = 1.0
            else:
                m[j, j + 1:split[j]] = 1.0
        seg.append(m)
        same = start[:, None] == start[None, :]
        causal = same & upper[:, None] & ~upper[None, :]
        anti = same & ~upper[:, None] & upper[None, :]
        mask_c.append(causal.astype(np.float32))
        mask_n.append((causal | (anti & (2 * w <= CHUNK))).astype(np.float32))
    tri_incl = (r[:, None] >= r[None, :]).astype(np.float32)
    seg_all = np.concatenate([seg[ws.index(w)] for w in SEG_WIDTHS] + [tri_incl], axis=0)

    hd = group // HEADS
    ch = np.arange(group)
    head_of = ch // hd
    avg = (head_of[:, None] == head_of[None, :]).astype(np.float32) / hd
    hm64 = (head_of[None, :] == np.arange(HEADS)[:, None]).astype(np.float32)[:, None, :]
    ck = np.arange(group // 2)
    head_of_k = ck // (hd // 2)
    hm32 = (head_of_k[None, :] == np.arange(HEADS)[:, None]).astype(np.float32)[:, None, :]
    bd_gla_t = (head_of[:, None] == head_of_k[None, :]).astype(np.float32)
    expand_gla = bd_gla_t.T.copy()

    log_gamma = np.log1p(-np.exp2(-5.0 - np.arange(HEADS, dtype=np.float64)))
    cj, cm = r[:, None] // CHUNK, r[None, :] // CHUNK
    dist = np.where(cj == cm, np.abs(r[:, None] - r[None, :]), r[:, None] - r[None, :])
    ret_mask = np.stack([np.where(cm <= cj, np.exp(dist * lg), 0.0) for lg in log_gamma])
    lg_ch = log_gamma[head_of]
    ret_qdec = np.exp((r[:, None] + 1) * lg_ch[None, :])
    ret_kdec = np.exp((sb - 1 - r[:, None]) * lg_ch[None, :])
    ret_sdec = np.exp(sb * lg_ch)[None, :]
    first_half = ((ch % hd) < hd // 2).astype(np.float32)[None, :]
    shift_all = np.concatenate(
        [(r[:, None] - k == r[None, :]).astype(np.float32) for k in range(1, CONV_W)], axis=0)
    return dict(
        seg_all=seg_all, mask_c=np.stack(mask_c), mask_n=np.stack(mask_n),
        avg=avg, hm64=hm64, hm32=hm32, bd64=(avg > 0).astype(np.float32),
        bd_gla_t=bd_gla_t, expand_gla=expand_gla,
        ret_mask=ret_mask.astype(np.float32), ret_qdec=ret_qdec.astype(np.float32),
        ret_kdec=ret_kdec.astype(np.float32), ret_sdec=ret_sdec.astype(np.float32),
        first_half=first_half, shift_all=shift_all,
    )


def _head_norm(o, avg_bf):
    mu = _dot(o.astype(BF16), avg_bf)
    dlt = o - mu
    var = _dot((dlt * dlt).astype(BF16), avg_bf)
    return dlt * lax.rsqrt(var + EPS)


def _gated_stages(q, k, v_bf, logf, seg_ref, mask_ref, hmq_ref, hmv_ref, b_ref, s_ref, st_ref,
                  diag_ref, bd_ref, sb, causal_in_chunk, out):
    dk = logf.shape[1]
    half = sb // 2
    logf = logf * LOG2_E
    hi, lo = _split_hi_lo(logf)
    both = _dot(seg_ref[...], jnp.concatenate([hi, lo], axis=1))
    sums = both[:, :dk] + both[:, dk:]
    fine = {w: sums[i * sb:(i + 1) * sb] for i, w in enumerate(SEG_WIDTHS)}
    b_incl = sums[len(SEG_WIDTHS) * sb:]
    b_ref[:, :dk] = b_incl
    odd = lax.broadcasted_iota(jnp.int32, (sb, dk), 0) % 2 == 1
    s_ref[...] = jnp.zeros_like(s_ref)
    yield
    for lvl, w in enumerate(_level_widths(sb)):
        if w in fine:
            d = fine[w]
        elif w == 1:
            d = jnp.where(odd, logf, 0.0)
        else:
            beta = jnp.concatenate(
                [jnp.broadcast_to(b_ref[gs + w - 1:gs + w, :dk], (2 * w, dk))
                 for gs in range(0, sb, 2 * w)], axis=0)
            d = -jnp.abs(b_incl - beta)
        e = jnp.exp2(d)
        ks = (k * e).astype(BF16)
        causal = causal_in_chunk or 2 * w > CHUNK
        if causal and w >= 8:
            runs = list(range(0, sb, 2 * w))
            take = lambda t: jnp.concatenate([t[gs + w:gs + 2 * w] for gs in runs], axis=0)
            qs = (take(q) * take(e)).astype(BF16)
            for pair in range(HEADS // 2):
                lhs = jnp.concatenate([qs * hmq_ref[2 * pair], qs * hmq_ref[2 * pair + 1]], axis=0)
                sc = _dot_nt(lhs, ks)
                for sub in range(2):
                    hh = 2 * pair + sub
                    for gi, gs in enumerate(runs):
                        c0 = (gs // LANE) * LANE
                        c1 = c0 + min(LANE, sb)
                        part = sc[sub * half + gi * w:sub * half + (gi + 1) * w, c0:c1]
                        if 2 * w < sb or w % LANE:
                            part = part * mask_ref[lvl, gs + w:gs + 2 * w, c0:c1]
                        s_ref[hh, gs + w:gs + 2 * w, c0:c1] += part
        else:
            qs = (q * e).astype(BF16)
            for hh in range(HEADS):
                sc = _dot_nt(qs * hmq_ref[hh], ks)
                for t in range(0, sb, LANE):
                    s_ref[hh, t:t + LANE, t:t + LANE] += (
                        sc[t:t + LANE, t:t + LANE] * mask_ref[lvl, t:t + LANE, t:t + LANE])
        yield
    b_after = b_ref[sb - 1:sb, :dk] - b_incl
    o = _dot((q * k).astype(BF16), diag_ref[...].astype(BF16)) * v_bf.astype(F32)
    o += _dot_nt((q * jnp.exp2(b_incl)).astype(BF16), st_ref[...].astype(BF16))
    yield
    for hh in range(HEADS):
        o += _dot(s_ref[hh].astype(BF16), v_bf * hmv_ref[hh])
    yield
    st_ref[...] = (st_ref[...] * jnp.exp2(b_incl[sb - 1:sb, :])
                   + _dot_tn(v_bf, (k * jnp.exp2(b_after)).astype(BF16)) * bd_ref[...])
    out.append(o)


def _mixer_body(p_ref, cos_ref, sin_ref,
                seg_ref, maskc_ref, maskn_ref, avg_ref, hm64_ref, hm32_ref, bd64_ref,
                bdgt_ref, expg_ref, rmask_ref, rqdec_ref, rkdec_ref, rsdec_ref, fh_ref, shift_ref,
                convw_ref, convb_ref, wa_ref, ba_ref, wx_ref, bx_ref, lam_ref,
                wa2_ref, bga_ref, lbl_ref,
                y_ref,
                bg_ref, bh_ref, sg_ref, sh_ref, ret_st, gla_st, hgrn_st, lru_h, lru_prev,
                *, sb, group, layer):
    g = group
    sidx = pl.program_id(1)

    @pl.when(sidx == 0)
    def _():
        ret_st[...] = jnp.zeros_like(ret_st)
        gla_st[...] = jnp.zeros_like(gla_st)
        hgrn_st[...] = jnp.zeros_like(hgrn_st)
        lru_h[...] = jnp.zeros_like(lru_h)
        lru_prev[...] = jnp.zeros_like(lru_prev)

    def col(i, width=g):
        return p_ref[0, :, i:i + width].astype(F32)

    avg_bf = avg_ref[...]

    def finish(o, gate, c0):
        y_ref[0, :, c0:c0 + g] = (_silu(gate) * _head_norm(o, avg_bf)).astype(y_ref.dtype)


    def retention():
        rq, rk, rv = col(0), col(g), p_ref[0, :, 2 * g:3 * g]
        cos, sin = cos_ref[...], sin_ref[...]
        fh = fh_ref[...] > 0.5
        hd = g // HEADS

        def rope(t):
            swapped = jnp.where(fh, pltpu.roll(t, g - hd // 2, 1), pltpu.roll(t, hd // 2, 1))
            return t * cos + swapped * sin

        qr = rope(rq)
        kr = (rope(rk) * (hd ** -0.5))
        qr_bf, kr_bf = qr.astype(BF16), kr.astype(BF16)
        yield
        o = _dot_nt((qr * rqdec_ref[...]).astype(BF16), ret_st[...].astype(BF16))
        for hh in range(HEADS):
            sc = _dot_nt(qr_bf * hm64_ref[hh], kr_bf) * rmask_ref[hh]
            o += _dot(sc.astype(BF16), rv * hm64_ref[hh])
            yield
        ret_st[...] = (ret_st[...] * rsdec_ref[...]
                       + _dot_tn(rv, (kr * rkdec_ref[...]).astype(BF16)) * bd64_ref[...])
        yield
        finish(o, col(3 * g), 0)

    def rglru():
        c0 = 4 * g
        lx_bf, lg = p_ref[0, :, c0:c0 + g], col(c0 + g)
        lx = lx_bf.astype(F32)
        shifted = _dot(shift_ref[...], lx_bf)
        tail = lru_prev[...]
        row8 = lax.broadcasted_iota(jnp.int32, tail.shape, 0)
        xc = lx * convw_ref[CONV_W - 1:CONV_W, :] + convb_ref[...]
        fix = jnp.zeros_like(tail)
        for sh in range(1, CONV_W):
            tap = convw_ref[CONV_W - 1 - sh:CONV_W - sh, :]
            xc += shifted[(sh - 1) * sb:sh * sb] * tap
            fix += jnp.where(row8 < sh, pltpu.roll(tail, sh, 0), 0.0) * tap
        xc = jnp.concatenate([xc[:8] + fix, xc[8:]], axis=0)
        lru_prev[...] = lx[sb - 8:sb]
        xc_bf = xc.astype(BF16)
        yield
        rgate = _sigmoid(_dot(xc_bf, wa_ref[...]) + ba_ref[...])
        igate = _sigmoid(_dot(xc_bf, wx_ref[...]) + bx_ref[...])
        log_a = (LRU_C * rgate) * _log_sigmoid(lam_ref[...])
        a = jnp.exp(log_a)
        u = jnp.sqrt(_neg_expm1(2.0 * log_a)) * (igate * xc)
        yield
        sh = 1
        while sh < sb:
            if sh % 8:
                keep = row8 >= sh
                a_r, u_r = pltpu.roll(a, sh, 0), pltpu.roll(u, sh, 0)
                a_sh = jnp.concatenate([jnp.where(keep, a_r[:8], 1.0), a_r[8:]], axis=0)
                u_sh = jnp.concatenate([jnp.where(keep, u_r[:8], 0.0), u_r[8:]], axis=0)
                u = u + a * u_sh
                a = a * a_sh
            else:
                u = jnp.concatenate([u[:sh], u[sh:] + a[sh:] * u[:sb - sh]], axis=0)
                a = jnp.concatenate([a[:sh], a[sh:] * a[:sb - sh]], axis=0)
            sh *= 2
            yield
        hseq = u + a * lru_h[...]
        lru_h[...] = hseq[sb - 1:sb, :]
        y_ref[0, :, g:2 * g] = (hseq * jax.nn.gelu(lg, approximate=True)).astype(y_ref.dtype)

    def gla():
        c0 = 6 * g
        dk = g // 2
        gq = col(c0, dk) * ((dk // HEADS) ** -0.5)
        gk = col(c0 + dk, dk)
        gv = p_ref[0, :, c0 + 2 * dk:c0 + 2 * dk + g]
        ga = p_ref[0, :, 13 * g:13 * g + LANE]
        a_pre = _dot(ga, wa2_ref[...]) + bga_ref[...]
        logf = _log_sigmoid(a_pre) / GLA_TAU
        out = []
        yield from _gated_stages(gq, gk, gv, logf, seg_ref, maskn_ref, hm32_ref, hm64_ref,
                                 bg_ref, sg_ref, gla_st, expg_ref, bdgt_ref, sb, False, out)
        yield
        finish(out[0], col(c0 + 2 * dk + g), 2 * g)

    def hgrn2():
        c0 = 9 * g
        hq, hf, hv = col(c0), col(c0 + g), p_ref[0, :, c0 + 2 * g:c0 + 3 * g]
        logits = lbl_ref[...]
        ex = jnp.exp(logits - jnp.max(logits, axis=0, keepdims=True))
        sm = ex / jnp.sum(ex, axis=0, keepdims=True)
        lb = jnp.zeros((1, g), F32)
        for i in range(1, layer + 1):
            lb = lb + sm[i:i + 1, :]
        logf = _logaddexp(jnp.log(lb), jnp.log1p(-lb) + _log_sigmoid(hf))
        hk = (1.0 - lb) * _sigmoid(-hf)
        hqs = _silu(hq)
        out = []
        yield from _gated_stages(hqs, hk, hv, logf, seg_ref, maskc_ref, hm64_ref, hm64_ref,
                                 bh_ref, sh_ref, hgrn_st, bd64_ref, bd64_ref, sb, True, out)
        yield
        finish(out[0], col(c0 + 3 * g), 3 * g)

    waiting = [[hgrn2(), gla()], [retention()], [rglru()]]
    live = []
    while live or waiting:
        if waiting:
            live += waiting.pop(0)
        live = [m for m in live if next(m, StopIteration) is not StopIteration]


def _mixers(proj, cos_t, sin_t, consts, lw, *, layer, sb, group):
    b, s, n = proj.shape
    g = group

    def full(a):
        nd = a.ndim
        return pl.BlockSpec(a.shape, lambda i, j, nd=nd: (0,) * nd)

    const_args = [
        consts["seg_all"].astype(BF16), consts["mask_c"], consts["mask_n"],
        consts["avg"].astype(BF16), consts["hm64"].astype(BF16), consts["hm32"].astype(BF16),
        consts["bd64"],
        consts["bd_gla_t"], consts["expand_gla"].astype(BF16), consts["ret_mask"],
        consts["ret_qdec"], consts["ret_kdec"], consts["ret_sdec"], consts["first_half"],
        consts["shift_all"].astype(BF16),
    ]
    layer_args = [
        lw["conv_w"], lw["conv_b"], lw["wa_bd"], lw["ba"], lw["wx_bd"], lw["bx"], lw["lam"],
        lw["w_a2"], lw["b_a"], lw["lb_logits"],
    ]
    in_specs = ([pl.BlockSpec((1, sb, n), lambda i, j: (i, j, 0)),
                 pl.BlockSpec((sb, g), lambda i, j: (j, 0)),
                 pl.BlockSpec((sb, g), lambda i, j: (j, 0))]
                + [full(a) for a in const_args] + [full(a) for a in layer_args])
    return pl.pallas_call(
        functools.partial(_mixer_body, sb=sb, group=g, layer=layer),
        grid=(b, s // sb),
        in_specs=in_specs,
        out_specs=pl.BlockSpec((1, sb, N_GROUPS * g), lambda i, j: (i, j, 0)),
        out_shape=jax.ShapeDtypeStruct((b, s, N_GROUPS * g), BF16),
        scratch_shapes=[
            pltpu.VMEM((sb, g), F32),
            pltpu.VMEM((sb, g), F32),
            pltpu.VMEM((HEADS, sb, sb), F32),
            pltpu.VMEM((HEADS, sb, sb), F32),
            pltpu.VMEM((g, g), F32),
            pltpu.VMEM((g, g // 2), F32),
            pltpu.VMEM((g, g), F32),
            pltpu.VMEM((1, g), F32),
            pltpu.VMEM((8, g), F32),
        ],
        compiler_params=pltpu.CompilerParams(
            dimension_semantics=("parallel", "arbitrary"), vmem_limit_bytes=VMEM_LIMIT),
        name="mixers",
    )(proj, cos_t, sin_t, *const_args, *layer_args)


def _block_diag(w):
    nb, bd, _ = w.shape
    out = jnp.zeros((nb * bd, nb * bd), w.dtype)
    for i in range(nb):
        out = out.at[i * bd:(i + 1) * bd, i * bd:(i + 1) * bd].set(w[i])
    return out


def kernel(x, mem, ffn1_norm, ffn1_w_gu, ffn1_w_down, mix_norm, w_in, w_out, lru_conv_w, lru_conv_b, lru_wa, lru_ba, lru_wx, lru_bx, lru_lambda, gla_w_a2, gla_b_a, hgrn_lb_logits, xattn_norm, mem_norm, xattn_wq, xattn_wkv, xattn_wo, ffn2_norm, ffn2_w_gu, ffn2_w_down, final_norm):
    bsz, seq, d = x.shape
    depth = w_in.shape[0]
    n_mem = mem.shape[1]
    g = d // N_GROUPS
    sb = min(256, seq)
    t = bsz * seq

    hd = g // HEADS
    inv_freq = ROPE_BASE ** (-jnp.arange(hd // 2, dtype=F32) / (hd // 2))
    ang = jnp.arange(seq, dtype=F32)[:, None] * inv_freq[None, :]
    cos_t = jnp.tile(jnp.cos(ang), (1, 2 * HEADS))
    sin_t = jnp.tile(jnp.concatenate([-jnp.sin(ang), jnp.sin(ang)], axis=-1), (1, HEADS))

    consts = {k: jnp.asarray(v) for k, v in _mixer_consts(sb, g).items()}

    ga0 = 4 * g + 2 * g + g // 2 + g // 2 + g
    ga1 = ga0 + GLA_RANK
    bf = lambda a: a.astype(BF16)
    xf = x.reshape(t, d)
    memf = mem.reshape(bsz * n_mem, d)
    w_in_bf = bf(jnp.concatenate(
        [w_in[:, :, :ga0], w_in[:, :, ga1:],
         jnp.pad(w_in[:, :, ga0:ga1], ((0, 0), (0, 0), (0, LANE - GLA_RANK)))], axis=2))
    gu1, dn1, gu2, dn2 = bf(ffn1_w_gu), bf(ffn1_w_down), bf(ffn2_w_gu), bf(ffn2_w_down)
    wkv_bf, wout_bf, wq_bf, wo_bf = bf(xattn_wkv), bf(w_out), bf(xattn_wq), bf(xattn_wo)
    for l in range(depth):
        lw = dict(
            conv_w=lru_conv_w[l], conv_b=lru_conv_b[l].reshape(1, g),
            wa_bd=bf(_block_diag(lru_wa[l])), ba=lru_ba[l].reshape(1, g),
            wx_bd=bf(_block_diag(lru_wx[l])), bx=lru_bx[l].reshape(1, g),
            lam=lru_lambda[l].reshape(1, g),
            w_a2=bf(jnp.pad(gla_w_a2[l], ((0, LANE - GLA_RANK), (0, 0)))),
            b_a=gla_b_a[l].reshape(1, g // 2),
            lb_logits=hgrn_lb_logits,
        )
        xf = _ffn(xf, ffn1_norm[l], gu1, dn1, l)
        proj = _normproj(xf, mix_norm[l], w_in_bf, l, tm=1024, name="in_proj")
        y = _mixers(proj.reshape(bsz, seq, -1), cos_t, sin_t, consts, lw,
                    layer=l, sb=sb, group=g)
        kv = _normproj(memf, mem_norm[l], wkv_bf, l, name="mem_kv")
        xf = _xattn(xf, y, kv.reshape(bsz, n_mem, 2 * d), wout_bf, xattn_norm[l],
                    wq_bf, wo_bf, l, seq=seq)
        xf = _ffn(xf, ffn2_norm[l], gu2, dn2, l,
                  final_g=final_norm if l == depth - 1 else None)
    return xf.reshape(bsz, seq, d)
```

```python
import functools

import numpy as np
import jax
import jax.numpy as jnp
from jax import lax
from jax.experimental import pallas as pl
from jax.experimental.pallas import tpu as pltpu

F32 = jnp.float32
BF16 = jnp.bfloat16

CHUNK = 64
N_GROUPS = 4
HEADS = 4
ROPE_BASE = 10000.0
CONV_W = 4
LRU_C = 8.0
GLA_RANK = 16
GLA_TAU = 16.0
EPS = 1e-6
LANE = 128

VMEM_LIMIT = 56 * 1024 * 1024


def _rms(x, g):
    return x * lax.rsqrt(jnp.mean(x * x, axis=-1, keepdims=True) + EPS) * g


def _dot(a, b):
    return jnp.dot(a, b, preferred_element_type=F32)


def _dot_nt(a, b):
    return lax.dot_general(a, b, (((1,), (1,)), ((), ())), preferred_element_type=F32)


def _dot_tn(a, b):
    return lax.dot_general(a, b, (((0,), (0,)), ((), ())), preferred_element_type=F32)


def _split_hi_lo(x):
    hi = x.astype(BF16)
    lo = (x - hi.astype(F32)).astype(BF16)
    return hi, lo


def _sigmoid(x):
    return 1.0 / (1.0 + jnp.exp(-x))


def _log_sigmoid(x):
    return jnp.minimum(x, 0.0) - jnp.log(1.0 + jnp.exp(-jnp.abs(x)))


def _neg_expm1(x):
    t = jnp.tanh(0.5 * x)
    return -2.0 * t / (1.0 - t)


def _logaddexp(a, b):
    return jnp.maximum(a, b) + jnp.log(1.0 + jnp.exp(-jnp.abs(a - b)))


def _silu(x):
    return x * _sigmoid(x)


def _ffn_body(x_ref, xn_ref, g_ref, wg_ref, wu_ref, wd_ref, *rest, final, rows, la_rows):
    if final:
        gf_ref, o_ref, h_ref = rest
    else:
        o_ref, h_ref = rest
    i, j = pl.program_id(0), pl.program_id(1)
    tm = x_ref.shape[0]
    n_la = tm // la_rows
    slot = i % 2

    @pl.when(jnp.logical_and(i == 0, j == 0))
    def _():
        h_ref[0] = _rms(x_ref[...], g_ref[...]).astype(BF16)

    def step(first, lookahead):
        for r0 in range(0, tm, rows):
            h = h_ref[slot, r0:r0 + rows, :]
            gate = _dot(h, wg_ref[...])
            up = _dot(h, wu_ref[...])
            act = (0.5 * _silu(gate) * up).astype(BF16)
            down = _dot(act, wd_ref[...])
            if first:
                o_ref[r0:r0 + rows, :] = x_ref[r0:r0 + rows, :] + down
            else:
                o_ref[r0:r0 + rows, :] += down
        if lookahead:
            off = pl.multiple_of(j * la_rows, la_rows)
            h_ref[1 - slot, pl.ds(off, la_rows), :] = (
                _rms(xn_ref[...], g_ref[...]).astype(BF16))

    pl.when(j == 0)(functools.partial(step, True, True))
    pl.when(jnp.logical_and(j > 0, j < n_la))(functools.partial(step, False, True))
    pl.when(j >= n_la)(functools.partial(step, False, False))

    if final:
        @pl.when(j == pl.num_programs(1) - 1)
        def _():
            o_ref[...] = _rms(o_ref[...], gf_ref[...])


def _ffn(x, g, w_gu, w_down, layer, final_g=None, *, tm=2048, tf=256, rows=1024, la_rows=256):
    t, d = x.shape
    f = w_down.shape[1]
    nj = f // tf
    tm = min(tm, t)
    rows = min(rows, tm)
    nt = t // tm
    n_la = tm // la_rows
    assert n_la <= nj and tm % rows == 0

    def next_slab(i, j):
        return (jnp.minimum(i + 1, nt - 1) * n_la + jnp.minimum(j, n_la - 1), 0)

    in_specs = [
        pl.BlockSpec((tm, d), lambda i, j: (i, 0)),
        pl.BlockSpec((la_rows, d), next_slab),
        pl.BlockSpec((1, d), lambda i, j: (0, 0)),
        pl.BlockSpec((None, d, tf), lambda i, j: (layer, 0, j)),
        pl.BlockSpec((None, d, tf), lambda i, j: (layer, 0, j + nj)),
        pl.BlockSpec((None, tf, d), lambda i, j: (layer, j, 0)),
    ]
    args = [x, x, g.reshape(1, d), w_gu, w_gu, w_down]
    if final_g is not None:
        in_specs.append(pl.BlockSpec((1, d), lambda i, j: (0, 0)))
        args.append(final_g.reshape(1, d))
    return pl.pallas_call(
        functools.partial(_ffn_body, final=final_g is not None, rows=rows, la_rows=la_rows),
        grid=(nt, nj),
        in_specs=in_specs,
        out_specs=pl.BlockSpec((tm, d), lambda i, j: (i, 0)),
        out_shape=jax.ShapeDtypeStruct((t, d), F32),
        scratch_shapes=[pltpu.VMEM((2, tm, d), BF16)],
        compiler_params=pltpu.CompilerParams(
            dimension_semantics=("arbitrary", "arbitrary"), vmem_limit_bytes=VMEM_LIMIT),
        name="ffn_final" if final_g is not None else "ffn",
    )(*args)


def _normproj_body(x_ref, g_ref, w_ref, o_ref):
    h = _rms(x_ref[...], g_ref[...]).astype(BF16)
    o_ref[...] = _dot(h, w_ref[...]).astype(o_ref.dtype)


def _normproj(x, g, w, layer, *, tm=512, name="normproj"):
    t, d = x.shape
    n = w.shape[2]
    return pl.pallas_call(
        _normproj_body,
        grid=(t // tm,),
        in_specs=[
            pl.BlockSpec((tm, d), lambda i: (i, 0)),
            pl.BlockSpec((1, d), lambda i: (0, 0)),
            pl.BlockSpec((None, d, n), lambda i: (layer, 0, 0)),
        ],
        out_specs=pl.BlockSpec((tm, n), lambda i: (i, 0)),
        out_shape=jax.ShapeDtypeStruct((t, n), BF16),
        compiler_params=pltpu.CompilerParams(
            dimension_semantics=("parallel",), vmem_limit_bytes=VMEM_LIMIT),
        name=name,
    )(x, g.reshape(1, d), w)


def _xattn_body(x_ref, y_ref, kv_ref, wout_ref, g_ref, wq_ref, wo_ref, o_ref):
    d = x_ref.shape[-1]
    hd = d // HEADS
    x1 = x_ref[...] + _dot(y_ref[0], wout_ref[...])
    h = _rms(x1, g_ref[...]).astype(BF16)
    q = _dot(h, wq_ref[...]).astype(BF16)
    kv = kv_ref[0]
    outs = []
    for hh in range(HEADS):
        qh = q[:, hh * hd:(hh + 1) * hd]
        kh = kv[:, hh * hd:(hh + 1) * hd]
        vh = kv[:, d + hh * hd:d + (hh + 1) * hd]
        s = _dot_nt(qh, kh) * (hd ** -0.5)
        s = s - jnp.max(s, axis=-1, keepdims=True)
        e = jnp.exp(s)
        p = e / jnp.sum(e, axis=-1, keepdims=True)
        outs.append(_dot(p.astype(BF16), vh).astype(BF16))
    o = jnp.concatenate(outs, axis=-1)
    o_ref[...] = x1 + _dot(o, wo_ref[...])


def _xattn(x, y, kv, w_out, g, wq, wo, layer, *, seq, tm=1024):
    t, d = x.shape
    b = t // seq
    tm = min(tm, seq)
    ns = seq // tm
    n_mem = kv.shape[1]
    wspec = pl.BlockSpec((None, d, d), lambda i, j: (layer, 0, 0))
    return pl.pallas_call(
        _xattn_body,
        grid=(b, ns),
        in_specs=[
            pl.BlockSpec((tm, d), lambda i, j: (i * ns + j, 0)),
            pl.BlockSpec((1, tm, d), lambda i, j: (i, j, 0)),
            pl.BlockSpec((1, n_mem, 2 * d), lambda i, j: (i, 0, 0)),
            wspec,
            pl.BlockSpec((1, d), lambda i, j: (0, 0)),
            wspec,
            wspec,
        ],
        out_specs=pl.BlockSpec((tm, d), lambda i, j: (i * ns + j, 0)),
        out_shape=jax.ShapeDtypeStruct((t, d), F32),
        compiler_params=pltpu.CompilerParams(
            dimension_semantics=("parallel", "parallel"), vmem_limit_bytes=VMEM_LIMIT),
        name="outproj_xattn",
    )(x, y, kv, w_out, g.reshape(1, d), wq, wo)


LOG2_E = 1.4426950408889634
SEG_WIDTHS = (2,)


def _level_widths(sb):
    ws, w = [], sb // 2
    while w >= 1:
        ws.append(w)
        w //= 2
    return ws


def _mixer_consts(sb, group):
    r = np.arange(sb)
    ws = _level_widths(sb)
    seg, mask_c, mask_n = [], [], []
    for w in ws:
        start = (r // (2 * w)) * (2 * w)
        split = start + w
        upper = r >= split
        m = np.zeros((sb, sb), np.float32)
        for j in range(sb):
            if upper[j]:
                m[j, split[j]:j + 1] = 1.0
            else:
                m[j, j + 1:split[j]] = 1.0
        seg.append(m)
        same = start[:, None] == start[None, :]
        causal = same & upper[:, None] & ~upper[None, :]
        anti = same & ~upper[:, None] & upper[None, :]
        mask_c.append(causal.astype(np.float32))
        mask_n.append((causal | (anti & (2 * w <= CHUNK))).astype(np.float32))
    tri_incl = (r[:, None] >= r[None, :]).astype(np.float32)
    seg_all = np.concatenate([seg[ws.index(w)] for w in SEG_WIDTHS] + [tri_incl], axis=0)

    hd = group // HEADS
    ch = np.arange(group)
    head_of = ch // hd
    avg = (head_of[:, None] == head_of[None, :]).astype(np.float32) / hd
    hm64 = (head_of[None, :] == np.arange(HEADS)[:, None]).astype(np.float32)[:, None, :]
    ck = np.arange(group // 2)
    head_of_k = ck // (hd // 2)
    hm32 = (head_of_k[None, :] == np.arange(HEADS)[:, None]).astype(np.float32)[:, None, :]
    bd_gla_t = (head_of[:, None] == head_of_k[None, :]).astype(np.float32)
    expand_gla = bd_gla_t.T.copy()

    log_gamma = np.log1p(-np.exp2(-5.0 - np.arange(HEADS, dtype=np.float64)))
    cj, cm = r[:, None] // CHUNK, r[None, :] // CHUNK
    dist = np.where(cj == cm, np.abs(r[:, None] - r[None, :]), r[:, None] - r[None, :])
    ret_mask = np.stack([np.where(cm <= cj, np.exp(dist * lg), 0.0) for lg in log_gamma])
    lg_ch = log_gamma[head_of]
    ret_qdec = np.exp((r[:, None] + 1) * lg_ch[None, :])
    ret_kdec = np.exp((sb - 1 - r[:, None]) * lg_ch[None, :])
    ret_sdec = np.exp(sb * lg_ch)[None, :]
    first_half = ((ch % hd) < hd // 2).astype(np.float32)[None, :]
    shift_all = np.concatenate(
        [(r[:, None] - k == r[None, :]).astype(np.float32) for k in range(1, CONV_W)], axis=0)
    return dict(
        seg_all=seg_all, mask_c=np.stack(mask_c), mask_n=np.stack(mask_n),
        avg=avg, hm64=hm64, hm32=hm32, bd64=(avg > 0).astype(np.float32),
        bd_gla_t=bd_gla_t, expand_gla=expand_gla,
        ret_mask=ret_mask.astype(np.float32), ret_qdec=ret_qdec.astype(np.float32),
        ret_kdec=ret_kdec.astype(np.float32), ret_sdec=ret_sdec.astype(np.float32),
        first_half=first_half, shift_all=shift_all,
    )


def _head_norm(o, avg_bf):
    mu = _dot(o.astype(BF16), avg_bf)
    dlt = o - mu
    var = _dot((dlt * dlt).astype(BF16), avg_bf)
    return dlt * lax.rsqrt(var + EPS)


def _gated_stages(q, k, v_bf, logf, seg_ref, mask_ref, hmq_ref, hmv_ref, b_ref, s_ref, st_ref,
                  diag_ref, bd_ref, sb, causal_in_chunk, out):
    dk = logf.shape[1]
    logf = logf * LOG2_E
    hi, lo = _split_hi_lo(logf)
    both = _dot(seg_ref[...], jnp.concatenate([hi, lo], axis=1))
    sums = both[:, :dk] + both[:, dk:]
    fine = {w: sums[i * sb:(i + 1) * sb] for i, w in enumerate(SEG_WIDTHS)}
    b_incl = sums[len(SEG_WIDTHS) * sb:]
    b_ref[:, :dk] = b_incl
    odd = lax.broadcasted_iota(jnp.int32, (sb, dk), 0) % 2 == 1
    s_ref[...] = jnp.zeros_like(s_ref)
    yield
    for lvl, w in enumerate(_level_widths(sb)):
        if w in fine:
            d = fine[w]
        elif w == 1:
            d = jnp.where(odd, logf, 0.0)
        else:
            beta = jnp.concatenate(
                [jnp.broadcast_to(b_ref[gs + w - 1:gs + w, :dk], (2 * w, dk))
                 for gs in range(0, sb, 2 * w)], axis=0)
            d = -jnp.abs(b_incl - beta)
        e = jnp.exp2(d)
        ks = (k * e).astype(BF16)
        causal = causal_in_chunk or 2 * w > CHUNK
        if causal and w >= 8:
            runs = list(range(0, sb, 2 * w))
            take = lambda t: jnp.concatenate([t[gs + w:gs + 2 * w] for gs in runs], axis=0)
            qs = (take(q) * take(e)).astype(BF16)
            span = max(2 * w, LANE)
            nq = span // 2
            for ti, t0 in enumerate(range(0, sb, span)):
                qt = qs[ti * nq:(ti + 1) * nq]
                kt = ks[t0:t0 + LANE]
                for pair in range(HEADS // 2):
                    lhs = jnp.concatenate([qt * hmq_ref[2 * pair], qt * hmq_ref[2 * pair + 1]],
                                          axis=0)
                    sc = _dot_nt(lhs, kt)
                    for sub in range(2):
                        hh = 2 * pair + sub
                        for gi, gs in enumerate(range(t0, t0 + span, 2 * w)):
                            part = sc[sub * nq + gi * w:sub * nq + (gi + 1) * w]
                            if w < LANE:
                                part = part * mask_ref[lvl, gs + w:gs + 2 * w, t0:t0 + LANE]
                            s_ref[hh, gs + w:gs + 2 * w, t0:t0 + LANE] += part
        else:
            qs = (q * e).astype(BF16)
            for hh in range(HEADS):
                qh = qs * hmq_ref[hh]
                for t in range(0, sb, LANE):
                    sc = _dot_nt(qh[t:t + LANE], ks[t:t + LANE])
                    s_ref[hh, t:t + LANE, t:t + LANE] += (
                        sc * mask_ref[lvl, t:t + LANE, t:t + LANE])
        yield
    b_after = b_ref[sb - 1:sb, :dk] - b_incl
    o = _dot((q * k).astype(BF16), diag_ref[...].astype(BF16)) * v_bf.astype(F32)
    o += _dot_nt((q * jnp.exp2(b_incl)).astype(BF16), st_ref[...].astype(BF16))
    yield
    for hh in range(HEADS):
        o += _dot(s_ref[hh].astype(BF16), v_bf * hmv_ref[hh])
    yield
    st_ref[...] = (st_ref[...] * jnp.exp2(b_incl[sb - 1:sb, :])
                   + _dot_tn(v_bf, (k * jnp.exp2(b_after)).astype(BF16)) * bd_ref[...])
    out.append(o)


def _mixer_body(p_ref, cos_ref, sin_ref,
                seg_ref, maskc_ref, maskn_ref, avg_ref, hm64_ref, hm32_ref, bd64_ref,
                bdgt_ref, expg_ref, rmask_ref, rqdec_ref, rkdec_ref, rsdec_ref, fh_ref, shift_ref,
                convw_ref, convb_ref, wa_ref, ba_ref, wx_ref, bx_ref, lam_ref,
                wa2_ref, bga_ref, lbl_ref,
                y_ref,
                bg_ref, bh_ref, sg_ref, sh_ref, ret_st, gla_st, hgrn_st, lru_h, lru_prev,
                *, sb, group, layer):
    g = group
    sidx = pl.program_id(1)

    @pl.when(sidx == 0)
    def _():
        ret_st[...] = jnp.zeros_like(ret_st)
        gla_st[...] = jnp.zeros_like(gla_st)
        hgrn_st[...] = jnp.zeros_like(hgrn_st)
        lru_h[...] = jnp.zeros_like(lru_h)
        lru_prev[...] = jnp.zeros_like(lru_prev)

    def col(i, width=g):
        return p_ref[0, :, i:i + width].astype(F32)

    avg_bf = avg_ref[...]

    def finish(o, gate, c0):
        y_ref[0, :, c0:c0 + g] = (_silu(gate) * _head_norm(o, avg_bf)).astype(y_ref.dtype)


    def retention():
        rq, rk, rv = col(0), col(g), p_ref[0, :, 2 * g:3 * g]
        cos, sin = cos_ref[...], sin_ref[...]
        fh = fh_ref[...] > 0.5
        hd = g // HEADS

        def rope(t):
            swapped = jnp.where(fh, pltpu.roll(t, g - hd // 2, 1), pltpu.roll(t, hd // 2, 1))
            return t * cos + swapped * sin

        qr = rope(rq)
        kr = (rope(rk) * (hd ** -0.5))
        qr_bf, kr_bf = qr.astype(BF16), kr.astype(BF16)
        yield
        o = _dot_nt((qr * rqdec_ref[...]).astype(BF16), ret_st[...].astype(BF16))
        for hh in range(HEADS):
            sc = _dot_nt(qr_bf * hm64_ref[hh], kr_bf) * rmask_ref[hh]
            o += _dot(sc.astype(BF16), rv * hm64_ref[hh])
            yield
        ret_st[...] = (ret_st[...] * rsdec_ref[...]
                       + _dot_tn(rv, (kr * rkdec_ref[...]).astype(BF16)) * bd64_ref[...])
        yield
        finish(o, col(3 * g), 0)

    def rglru():
        c0 = 4 * g
        lx_bf, lg = p_ref[0, :, c0:c0 + g], col(c0 + g)
        lx = lx_bf.astype(F32)
        shifted = _dot(shift_ref[...], lx_bf)
        tail = lru_prev[...]
        row8 = lax.broadcasted_iota(jnp.int32, tail.shape, 0)
        xc = lx * convw_ref[CONV_W - 1:CONV_W, :] + convb_ref[...]
        fix = jnp.zeros_like(tail)
        for sh in range(1, CONV_W):
            tap = convw_ref[CONV_W - 1 - sh:CONV_W - sh, :]
            xc += shifted[(sh - 1) * sb:sh * sb] * tap
            fix += jnp.where(row8 < sh, pltpu.roll(tail, sh, 0), 0.0) * tap
        xc = jnp.concatenate([xc[:8] + fix, xc[8:]], axis=0)
        lru_prev[...] = lx[sb - 8:sb]
        xc_bf = xc.astype(BF16)
        yield
        rgate = _sigmoid(_dot(xc_bf, wa_ref[...]) + ba_ref[...])
        igate = _sigmoid(_dot(xc_bf, wx_ref[...]) + bx_ref[...])
        log_a = (LRU_C * rgate) * _log_sigmoid(lam_ref[...])
        a = jnp.exp(log_a)
        u = jnp.sqrt(_neg_expm1(2.0 * log_a)) * (igate * xc)
        yield
        for sh in (1, 2, 4):
            keep = row8 >= sh
            a_r, u_r = pltpu.roll(a, sh, 0), pltpu.roll(u, sh, 0)
            a_sh = jnp.concatenate([jnp.where(keep, a_r[:8], 1.0), a_r[8:]], axis=0)
            u_sh = jnp.concatenate([jnp.where(keep, u_r[:8], 0.0), u_r[8:]], axis=0)
            u = u + a * u_sh
            a = a * a_sh
            yield
        tiles = []
        prev = jnp.broadcast_to(lru_h[...], (8, g))
        for t0 in range(0, sb, 8):
            prev = u[t0:t0 + 8] + a[t0:t0 + 8] * prev
            tiles.append(prev)
            if t0 % 64 == 56:
                yield
        hseq = jnp.concatenate(tiles, axis=0)
        lru_h[...] = hseq[sb - 1:sb, :]
        y_ref[0, :, g:2 * g] = (hseq * jax.nn.gelu(lg, approximate=True)).astype(y_ref.dtype)

    def gla():
        c0 = 6 * g
        dk = g // 2
        gq = col(c0, dk) * ((dk // HEADS) ** -0.5)
        gk = col(c0 + dk, dk)
        gv = p_ref[0, :, c0 + 2 * dk:c0 + 2 * dk + g]
        ga = p_ref[0, :, 13 * g:13 * g + LANE]
        a_pre = _dot(ga, wa2_ref[...]) + bga_ref[...]
        logf = _log_sigmoid(a_pre) / GLA_TAU
        out = []
        yield from _gated_stages(gq, gk, gv, logf, seg_ref, maskn_ref, hm32_ref, hm64_ref,
                                 bg_ref, sg_ref, gla_st, expg_ref, bdgt_ref, sb, False, out)
        yield
        finish(out[0], col(c0 + 2 * dk + g), 2 * g)

    def hgrn2():
        c0 = 9 * g
        hq, hf, hv = col(c0), col(c0 + g), p_ref[0, :, c0 + 2 * g:c0 + 3 * g]
        logits = lbl_ref[...]
        ex = jnp.exp(logits - jnp.max(logits, axis=0, keepdims=True))
        sm = ex / jnp.sum(ex, axis=0, keepdims=True)
        lb = jnp.zeros((1, g), F32)
        for i in range(1, layer + 1):
            lb = lb + sm[i:i + 1, :]
        logf = _logaddexp(jnp.log(lb), jnp.log1p(-lb) + _log_sigmoid(hf))
        hk = (1.0 - lb) * _sigmoid(-hf)
        hqs = _silu(hq)
        out = []
        yield from _gated_stages(hqs, hk, hv, logf, seg_ref, maskc_ref, hm64_ref, hm64_ref,
                                 bh_ref, sh_ref, hgrn_st, bd64_ref, bd64_ref, sb, True, out)
        yield
        finish(out[0], col(c0 + 3 * g), 3 * g)

    waiting = [[hgrn2(), gla()], [retention()], [rglru()]]
    live = []
    while live or waiting:
        if waiting:
            live += waiting.pop(0)
        live = [m for m in live if next(m, StopIteration) is not StopIteration]


def _mixers(proj, cos_t, sin_t, consts, lw, *, layer, sb, group):
    b, s, n = proj.shape
    g = group

    def full(a):
        nd = a.ndim
        return pl.BlockSpec(a.shape, lambda i, j, nd=nd: (0,) * nd)

    const_args = [
        consts["seg_all"].astype(BF16), consts["mask_c"], consts["mask_n"],
        consts["avg"].astype(BF16), consts["hm64"].astype(BF16), consts["hm32"].astype(BF16),
        consts["bd64"],
        consts["bd_gla_t"], consts["expand_gla"].astype(BF16), consts["ret_mask"],
        consts["ret_qdec"], consts["ret_kdec"], consts["ret_sdec"], consts["first_half"],
        consts["shift_all"].astype(BF16),
    ]
    layer_args = [
        lw["conv_w"], lw["conv_b"], lw["wa_bd"], lw["ba"], lw["wx_bd"], lw["bx"], lw["lam"],
        lw["w_a2"], lw["b_a"], lw["lb_logits"],
    ]
    in_specs = ([pl.BlockSpec((1, sb, n), lambda i, j: (i, j, 0)),
                 pl.BlockSpec((sb, g), lambda i, j: (j, 0)),
                 pl.BlockSpec((sb, g), lambda i, j: (j, 0))]
                + [full(a) for a in const_args] + [full(a) for a in layer_args])
    return pl.pallas_call(
        functools.partial(_mixer_body, sb=sb, group=g, layer=layer),
        grid=(b, s // sb),
        in_specs=in_specs,
        out_specs=pl.BlockSpec((1, sb, N_GROUPS * g), lambda i, j: (i, j, 0)),
        out_shape=jax.ShapeDtypeStruct((b, s, N_GROUPS * g), BF16),
        scratch_shapes=[
            pltpu.VMEM((sb, g), F32),
            pltpu.VMEM((sb, g), F32),
            pltpu.VMEM((HEADS, sb, sb), F32),
            pltpu.VMEM((HEADS, sb, sb), F32),
            pltpu.VMEM((g, g), F32),
            pltpu.VMEM((g, g // 2), F32),
            pltpu.VMEM((g, g), F32),
            pltpu.VMEM((1, g), F32),
            pltpu.VMEM((8, g), F32),
        ],
        compiler_params=pltpu.CompilerParams(
            dimension_semantics=("parallel", "arbitrary"), vmem_limit_bytes=VMEM_LIMIT),
        name="mixers",
    )(proj, cos_t, sin_t, *const_args, *layer_args)


def _block_diag(w):
    nb, bd, _ = w.shape
    out = jnp.zeros((nb * bd, nb * bd), w.dtype)
    for i in range(nb):
        out = out.at[i * bd:(i + 1) * bd, i * bd:(i + 1) * bd].set(w[i])
    return out


def kernel(x, mem, ffn1_norm, ffn1_w_gu, ffn1_w_down, mix_norm, w_in, w_out, lru_conv_w, lru_conv_b, lru_wa, lru_ba, lru_wx, lru_bx, lru_lambda, gla_w_a2, gla_b_a, hgrn_lb_logits, xattn_norm, mem_norm, xattn_wq, xattn_wkv, xattn_wo, ffn2_norm, ffn2_w_gu, ffn2_w_down, final_norm):
    bsz, seq, d = x.shape
    depth = w_in.shape[0]
    n_mem = mem.shape[1]
    g = d // N_GROUPS
    sb = min(256, seq)
    t = bsz * seq

    hd = g // HEADS
    inv_freq = ROPE_BASE ** (-jnp.arange(hd // 2, dtype=F32) / (hd // 2))
    ang = jnp.arange(seq, dtype=F32)[:, None] * inv_freq[None, :]
    cos_t = jnp.tile(jnp.cos(ang), (1, 2 * HEADS))
    sin_t = jnp.tile(jnp.concatenate([-jnp.sin(ang), jnp.sin(ang)], axis=-1), (1, HEADS))

    consts = {k: jnp.asarray(v) for k, v in _mixer_consts(sb, g).items()}

    ga0 = 4 * g + 2 * g + g // 2 + g // 2 + g
    ga1 = ga0 + GLA_RANK
    bf = lambda a: a.astype(BF16)
    xf = x.reshape(t, d)
    memf = mem.reshape(bsz * n_mem, d)
    w_in_bf = bf(jnp.concatenate(
        [w_in[:, :, :ga0], w_in[:, :, ga1:],
         jnp.pad(w_in[:, :, ga0:ga1], ((0, 0), (0, 0), (0, LANE - GLA_RANK)))], axis=2))
    gu1, dn1, gu2, dn2 = bf(ffn1_w_gu), bf(ffn1_w_down), bf(ffn2_w_gu), bf(ffn2_w_down)
    wkv_bf, wout_bf, wq_bf, wo_bf = bf(xattn_wkv), bf(w_out), bf(xattn_wq), bf(xattn_wo)
    for l in range(depth):
        lw = dict(
            conv_w=lru_conv_w[l], conv_b=lru_conv_b[l].reshape(1, g),
            wa_bd=bf(_block_diag(lru_wa[l])), ba=lru_ba[l].reshape(1, g),
            wx_bd=bf(_block_diag(lru_wx[l])), bx=lru_bx[l].reshape(1, g),
            lam=lru_lambda[l].reshape(1, g),
            w_a2=bf(jnp.pad(gla_w_a2[l], ((0, LANE - GLA_RANK), (0, 0)))),
            b_a=gla_b_a[l].reshape(1, g // 2),
            lb_logits=hgrn_lb_logits,
        )
        xf = _ffn(xf, ffn1_norm[l], gu1, dn1, l)
        proj = _normproj(xf, mix_norm[l], w_in_bf, l, tm=1024, name="in_proj")
        y = _mixers(proj.reshape(bsz, seq, -1), cos_t, sin_t, consts, lw,
                    layer=l, sb=sb, group=g)
        kv = _normproj(memf, mem_norm[l], wkv_bf, l, name="mem_kv")
        xf = _xattn(xf, y, kv.reshape(bsz, n_mem, 2 * d), wout_bf, xattn_norm[l],
                    wq_bf, wo_bf, l, seq=seq)
        xf = _ffn(xf, ffn2_norm[l], gu2, dn2, l,
                  final_g=final_norm if l == depth - 1 else None)
    return xf.reshape(bsz, seq, d)
```
